```python
import jax, jax.numpy as jnp
from jax import lax
import numpy as np


D_MODEL = 2048
BATCH = 2
SEQ = 4096
DEPTH = 2

D_A = D_MODEL // 2
A_HEAD = 128
A_HEADS = D_A // A_HEAD
A_CHUNK = 128
D_B = D_MODEL // 2
B_GROUPS = 4
B_GROUP = D_B // B_GROUPS
B_WINDOWS = (2, 4, 8, 16)
D_C = D_MODEL
C_HEAD = 128
C_HEADS = D_C // C_HEAD
C_CHUNK = 64
D_FF = 5632
CONV_W = 3
N_EVEN = (DEPTH + 1) // 2
N_ODD = DEPTH // 2
ALPHA = (2 * DEPTH) ** 0.25
BETA = (8 * DEPTH) ** -0.25
LN_EPS = 1e-5

kernel_name = 'hybrid_gmlp_pool_hgrn2_convffn_deepnorm'


def layer_norm(x, g, b):
    xf = x.astype(jnp.float32)
    mu = jnp.mean(xf, axis=-1, keepdims=True)
    var = jnp.mean(jnp.square(xf - mu), axis=-1, keepdims=True)
    return ((xf - mu) * lax.rsqrt(var + LN_EPS) * g + b).astype(x.dtype)


def rms_norm(x, g):
    xf = x.astype(jnp.float32)
    return xf * lax.rsqrt(jnp.mean(jnp.square(xf), axis=-1, keepdims=True) + LN_EPS) * g


def shift_right(x, s):
    pad = [(0, 0)] * x.ndim
    pad[1] = (s, 0)
    return jnp.pad(x, pad)[:, :x.shape[1]]


def spatial_gating(za, ln_g, ln_b, w_s, b_s):
    bn, t, _ = za.shape
    u, v = jnp.split(za, 2, axis=-1)
    v = layer_norm(v, ln_g, ln_b)
    v = v.reshape(bn, t // A_CHUNK, A_CHUNK, A_HEADS, A_HEAD)
    w = jnp.tril(w_s)
    s = jnp.einsum('hts,bnshc->bnthc', w, v) + b_s.T[None, None, :, :, None]
    return u * s.reshape(bn, t, D_A)


def multiscale_pool(xb, w_pool, scale):
    bn, t, _ = xb.shape
    xg = xb.reshape(bn, t, B_GROUPS, B_GROUP).astype(jnp.float32)
    csum = jnp.cumsum(xg, axis=1)
    pos = jnp.arange(1, t + 1, dtype=jnp.float32)
    outs = []
    for gi, win in enumerate(B_WINDOWS):
        c = csum[:, :, gi]
        wsum = c - shift_right(c, win)
        cnt = jnp.minimum(pos, float(win))[None, :, None]
        outs.append(wsum / cnt - xg[:, :, gi])
    p = jnp.stack(outs, axis=2).astype(xb.dtype)
    y = jnp.einsum('btgc,gcd->btgd', p, w_pool)
    return y.reshape(bn, t, D_B) * scale


def hgrn2(q, f_logit, inp, lb):
    bn, t, _ = q.shape
    n = t // C_CHUNK
    f32 = jnp.float32

    def heads(a):
        return a.astype(f32).reshape(bn, n, C_CHUNK, C_HEADS, C_HEAD).transpose(0, 3, 1, 2, 4)

    f = lb + (1.0 - lb) * jax.nn.sigmoid(f_logit.astype(f32))
    qh = heads(jax.nn.silu(q.astype(f32)))
    kh = heads(1.0 - f)
    vh = heads(inp)
    bcum = jnp.cumsum(heads(jnp.log(f)), axis=3)
    blast = bcum[:, :, :, -1:, :]
    q_dec = qh * jnp.exp(bcum)
    k_dec = kh * jnp.exp(-bcum)
    k_end = kh * jnp.exp(blast - bcum)
    mask = jnp.tril(jnp.ones((C_CHUNK, C_CHUNK), dtype=bool))
    att = jnp.where(mask, jnp.einsum('bhntd,bhnsd->bhnts', q_dec, k_dec), 0.0)
    o_intra = jnp.einsum('bhnts,bhnse->bhnte', att, vh)
    upd = jnp.einsum('bhnsd,bhnse->bhnde', k_end, vh)
    dec = jnp.exp(blast[:, :, :, 0, :])

    def step(state, xs):
        d_n, u_n = xs
        return d_n[..., None] * state + u_n, state

    s0 = jnp.zeros((bn, C_HEADS, C_HEAD, C_HEAD), f32)
    _, s_prev = lax.scan(step, s0, (jnp.moveaxis(dec, 2, 0), jnp.moveaxis(upd, 2, 0)))
    s_prev = jnp.moveaxis(s_prev, 0, 2)
    o = o_intra + jnp.einsum('bhntd,bhnde->bhnte', q_dec, s_prev)
    return o.transpose(0, 2, 3, 1, 4).reshape(bn, t, C_HEADS, C_HEAD)


def conv_ffn(x, w_up, conv_w, conv_b, w_down):
    h = x @ w_up
    hc = conv_b + conv_w[CONV_W - 1] * h
    for j in range(CONV_W - 1):
        hc = hc + conv_w[j] * shift_right(h, CONV_W - 1 - j)
    a, v = jnp.split(hc, 2, axis=-1)
    return (jax.nn.silu(a) * v) @ w_down


def setup_inputs(seed: int = 0) -> dict:
    key = jax.random.key(seed)
    ks = jax.random.split(key, 24)
    f32 = jnp.float32

    def nrm(k, shape, scale):
        return jax.random.normal(k, shape, f32) * scale

    return {
        'x': nrm(ks[0], (BATCH, SEQ, D_MODEL), 1.0),
        'ev_w_in': nrm(ks[1], (N_EVEN, D_MODEL, 2 * D_A + D_B), D_MODEL ** -0.5),
        'ev_ln_v_g': 1.0 + nrm(ks[2], (N_EVEN, D_A), 0.02),
        'ev_ln_v_b': nrm(ks[3], (N_EVEN, D_A), 0.02),
        'ev_w_s': nrm(ks[4], (N_EVEN, A_HEADS, A_CHUNK, A_CHUNK), 0.5 * A_CHUNK ** -0.5),
        'ev_b_s': 1.0 + nrm(ks[5], (N_EVEN, A_HEADS, A_CHUNK), 0.02),
        'ev_w_pool': nrm(ks[6], (N_EVEN, B_GROUPS, B_GROUP, B_GROUP), B_GROUP ** -0.5),
        'ev_pool_scale': 1.0 + nrm(ks[7], (N_EVEN, D_B), 0.02),
        'ev_w_out': nrm(ks[8], (N_EVEN, D_A + D_B, D_MODEL), BETA * (D_A + D_B) ** -0.5),
        'od_w_in': nrm(ks[9], (N_ODD, D_MODEL, 4 * D_C), D_MODEL ** -0.5),
        'od_norm_g': 1.0 + nrm(ks[10], (N_ODD, D_C), 0.02),
        'od_w_out': nrm(ks[11], (N_ODD, D_C, D_MODEL), BETA * D_C ** -0.5),
        'lb_param': nrm(ks[12], (DEPTH, D_C), 0.1),
        'ffn_w_up': nrm(ks[13], (DEPTH, D_MODEL, 2 * D_FF), D_MODEL ** -0.5),
        'ffn_conv_w': nrm(ks[14], (DEPTH, CONV_W, 2 * D_FF), CONV_W ** -0.5),
        'ffn_conv_b': nrm(ks[15], (DEPTH, 2 * D_FF), 0.02),
        'ffn_w_down': nrm(ks[16], (DEPTH, D_FF, D_MODEL), BETA * D_FF ** -0.5),
        'ln1_g': 1.0 + nrm(ks[17], (DEPTH, D_MODEL), 0.02),
        'ln1_b': nrm(ks[18], (DEPTH, D_MODEL), 0.02),
        'ln2_g': 1.0 + nrm(ks[19], (DEPTH, D_MODEL), 0.02),
        'ln2_b': nrm(ks[20], (DEPTH, D_MODEL), 0.02),
    }


def reference(x, ev_w_in, ev_ln_v_g, ev_ln_v_b, ev_w_s, ev_b_s, ev_w_pool, ev_pool_scale,
              ev_w_out, od_w_in, od_norm_g, od_w_out, lb_param, ffn_w_up, ffn_conv_w,
              ffn_conv_b, ffn_w_down, ln1_g, ln1_b, ln2_g, ln2_b):
    bn, t, _ = x.shape
    lb_all = jnp.cumsum(jax.nn.softmax(lb_param.astype(jnp.float32), axis=0), axis=0)
    lb_all = lb_all - lb_all[0]
    for l in range(DEPTH):
        if l % 2 == 0:
            e = l // 2
            h = x @ ev_w_in[e]
            za = jax.nn.gelu(h[..., :2 * D_A])
            xb = h[..., 2 * D_A:]
            ya = spatial_gating(za, ev_ln_v_g[e], ev_ln_v_b[e], ev_w_s[e], ev_b_s[e])
            yb = multiscale_pool(xb, ev_w_pool[e], ev_pool_scale[e])
            mix = jnp.concatenate([ya, yb], axis=-1) @ ev_w_out[e]
        else:
            o = l // 2
            h = x @ od_w_in[o]
            q, f_logit, inp, g = jnp.split(h, 4, axis=-1)
            y = hgrn2(q, f_logit, inp, lb_all[l])
            y = rms_norm(y, od_norm_g[o].reshape(C_HEADS, C_HEAD)).reshape(bn, t, D_C)
            y = (y * jax.nn.sigmoid(g.astype(jnp.float32))).astype(x.dtype)
            mix = y @ od_w_out[o]
        x = layer_norm(ALPHA * x + mix, ln1_g[l], ln1_b[l])
        x = layer_norm(ALPHA * x + conv_ffn(x, ffn_w_up[l], ffn_conv_w[l], ffn_conv_b[l], ffn_w_down[l]),
                       ln2_g[l], ln2_b[l])
    return x
```

```python
import functools

import jax
import jax.numpy as jnp
from jax import lax
from jax.experimental import pallas as pl
from jax.experimental.pallas import tpu as pltpu

F32 = jnp.float32
BF16 = jnp.bfloat16

A_HEAD = 128
B_WINDOWS = (2, 4, 8, 16)
C_HEAD = 128
C_CHUNK = 64
LN_EPS = 1e-5

SUBLANES = 8
VMEM_LIMIT_BYTES = 56 * 1024 * 1024


def _dot(a, b):
    return jnp.dot(a, b, preferred_element_type=F32)


def _dot_nt(a, b):
    return lax.dot_general(a, b, (((1,), (1,)), ((), ())), preferred_element_type=F32)


def _dot_tn(a, b):
    return lax.dot_general(a, b, (((0,), (0,)), ((), ())), preferred_element_type=F32)


def _layer_norm(z, g, b):
    mu = jnp.mean(z, axis=-1, keepdims=True)
    zc = z - mu
    var = jnp.mean(jnp.square(zc), axis=-1, keepdims=True)
    return zc * lax.rsqrt(var + LN_EPS) * g + b


def _proj_ln_kernel(y_ref, w_ref, x_ref, g_ref, b_ref, o_ref, *, alpha):
    mix = _dot(y_ref[...], w_ref[...])
    o_ref[...] = _layer_norm(alpha * x_ref[...] + mix, g_ref[...], b_ref[...])


def _proj_ln(y, w, x, g, b, *, alpha, tm):
    n, d = x.shape
    k = y.shape[1]
    return pl.pallas_call(
        functools.partial(_proj_ln_kernel, alpha=alpha),
        grid=(n // tm,),
        in_specs=[
            pl.BlockSpec((tm, k), lambda i: (i, 0)),
            pl.BlockSpec((k, d), lambda i: (0, 0)),
            pl.BlockSpec((tm, d), lambda i: (i, 0)),
            pl.BlockSpec((1, d), lambda i: (0, 0)),
            pl.BlockSpec((1, d), lambda i: (0, 0)),
        ],
        out_specs=pl.BlockSpec((tm, d), lambda i: (i, 0)),
        out_shape=jax.ShapeDtypeStruct((n, d), F32),
        compiler_params=pltpu.CompilerParams(
            dimension_semantics=("arbitrary",), vmem_limit_bytes=VMEM_LIMIT_BYTES),
        name="proj_ln",
    )(y, w, x, g.reshape(1, d), b.reshape(1, d))


def _conv_ffn_kernel(x_ref, wa_ref, wv_ref, cwa_ref, cwv_ref, cba_ref, cbv_ref, wd_ref,
                     g_ref, b_ref, o_ref, xa_ref, acc_ref, ha_ref, hv_ref, ca_ref, cv_ref,
                     *, alpha, tiles_per_seq, conv_w):
    i = pl.program_id(0)
    j = pl.program_id(1)
    tm = x_ref.shape[0]
    halo = SUBLANES

    @pl.when(j == 0)
    def _():
        xa_ref[...] = x_ref[...].astype(BF16)

    xa = xa_ref[...]

    def conv_half(w_ref, cw_ref, cb_ref, h_ref, carry_ref):
        h = _dot(xa, w_ref[...])

        @pl.when(i % tiles_per_seq == 0)
        def _():
            h_ref[0:halo, :] = jnp.zeros((halo, h.shape[1]), F32)

        @pl.when(i % tiles_per_seq != 0)
        def _():
            h_ref[0:halo, :] = carry_ref[j]

        h_ref[halo:halo + tm, :] = h
        carry_ref[j] = h[tm - halo:, :]
        cw = cw_ref[...]
        hc = cb_ref[...] + cw[conv_w - 1:conv_w, :] * h
        for tap in range(conv_w - 1):
            s = conv_w - 1 - tap
            hc = hc + cw[tap:tap + 1, :] * h_ref[halo - s:halo - s + tm, :]
        return hc

    a = conv_half(wa_ref, cwa_ref, cba_ref, ha_ref, ca_ref)
    v = conv_half(wv_ref, cwv_ref, cbv_ref, hv_ref, cv_ref)
    gated = (jax.nn.silu(a) * v).astype(BF16)
    d = _dot(gated, wd_ref[...])

    @pl.when(j == 0)
    def _():
        acc_ref[...] = d

    @pl.when(j != 0)
    def _():
        acc_ref[...] += d

    @pl.when(j == pl.num_programs(1) - 1)
    def _():
        o_ref[...] = _layer_norm(alpha * x_ref[...] + acc_ref[...], g_ref[...], b_ref[...])


def _conv_ffn(x, w_up, conv_w, conv_b, w_down, g, b, *, alpha, seq, tm, tf):
    n, d = x.shape
    d_ff = w_down.shape[0]
    nf = d_ff // tf
    taps = conv_w.shape[0]
    halo = SUBLANES
    assert taps - 1 <= halo and seq % tm == 0 and d_ff % tf == 0
    kern = functools.partial(_conv_ffn_kernel, alpha=alpha, tiles_per_seq=seq // tm, conv_w=taps)
    return pl.pallas_call(
        kern,
        grid=(n // tm, nf),
        in_specs=[
            pl.BlockSpec((tm, d), lambda i, j: (i, 0)),
            pl.BlockSpec((d, tf), lambda i, j: (0, j)),
            pl.BlockSpec((d, tf), lambda i, j: (0, nf + j)),
            pl.BlockSpec((taps, tf), lambda i, j: (0, j)),
            pl.BlockSpec((taps, tf), lambda i, j: (0, nf + j)),
            pl.BlockSpec((1, tf), lambda i, j: (0, j)),
            pl.BlockSpec((1, tf), lambda i, j: (0, nf + j)),
            pl.BlockSpec((tf, d), lambda i, j: (j, 0)),
            pl.BlockSpec((1, d), lambda i, j: (0, 0)),
            pl.BlockSpec((1, d), lambda i, j: (0, 0)),
        ],
        out_specs=pl.BlockSpec((tm, d), lambda i, j: (i, 0)),
        out_shape=jax.ShapeDtypeStruct((n, d), F32),
        scratch_shapes=[
            pltpu.VMEM((tm, d), BF16),
            pltpu.VMEM((tm, d), F32),
            pltpu.VMEM((halo + tm, tf), F32),
            pltpu.VMEM((halo + tm, tf), F32),
            pltpu.VMEM((nf, halo, tf), F32),
            pltpu.VMEM((nf, halo, tf), F32),
        ],
        compiler_params=pltpu.CompilerParams(
            dimension_semantics=("arbitrary", "arbitrary"), vmem_limit_bytes=VMEM_LIMIT_BYTES),
        name="conv_ffn",
    )(x, w_up, w_up, conv_w, conv_w, conv_b.reshape(1, -1), conv_b.reshape(1, -1), w_down,
      g.reshape(1, d), b.reshape(1, d))


def _even_mixer_kernel(x_ref, win_ref, lng_ref, lnb_ref, ws_ref, bst_ref, wp_ref, ps_ref,
                       y_ref, ext_ref, *, tiles_per_seq, d_a):
    i = pl.program_id(0)
    tm = x_ref.shape[0]
    n_heads = ws_ref.shape[0]
    chunk = ws_ref.shape[1]
    n_groups = wp_ref.shape[0]
    grp = wp_ref.shape[1]
    halo = ext_ref.shape[0] - tm

    h = _dot(x_ref[...].astype(BF16), win_ref[...])
    za = jax.nn.gelu(h[:, :2 * d_a])
    u = za[:, :d_a]
    v = _layer_norm(za[:, d_a:], lng_ref[...], lnb_ref[...]).astype(BF16)

    row = lax.broadcasted_iota(jnp.int32, (chunk, chunk), 0)
    col = lax.broadcasted_iota(jnp.int32, (chunk, chunk), 1)
    causal = col <= row
    bst = bst_ref[...]
    for hd in range(n_heads):
        w = jnp.where(causal, ws_ref[hd], 0.0).astype(BF16)
        cs = slice(hd * A_HEAD, (hd + 1) * A_HEAD)
        for c in range(tm // chunk):
            rs = slice(c * chunk, (c + 1) * chunk)
            s = _dot(w, v[rs, cs]) + bst[:, hd:hd + 1]
            y_ref[rs, cs] = (u[rs, cs] * s).astype(BF16)

    xb = h[:, 2 * d_a:]

    @pl.when(i % tiles_per_seq == 0)
    def _():
        ext_ref[0:halo, :] = jnp.zeros((halo, xb.shape[1]), F32)

    @pl.when(i % tiles_per_seq != 0)
    def _():
        ext_ref[0:halo, :] = ext_ref[tm:tm + halo, :]

    ext_ref[halo:halo + tm, :] = xb
    pos = (i % tiles_per_seq) * tm + lax.broadcasted_iota(jnp.int32, (tm, 1), 0)
    posf = (pos + 1).astype(F32)
    for gi in range(n_groups):
        win = B_WINDOWS[gi]
        cs = slice(gi * grp, (gi + 1) * grp)
        wsum = xb[:, cs]
        for k in range(1, win):
            wsum = wsum + ext_ref[halo - k:halo - k + tm, cs]
        cnt = jnp.minimum(posf, float(win))
        p = wsum / cnt - xb[:, cs]
        yb = _dot(p.astype(BF16), wp_ref[gi]) * ps_ref[:, cs]
        y_ref[:, d_a + gi * grp:d_a + (gi + 1) * grp] = yb.astype(BF16)


def _even_mixer(x, w_in, ln_g, ln_b, w_s, b_s, w_pool, pool_scale, *, seq, tm):
    n, d = x.shape
    d_a = ln_g.shape[0]
    d_b = pool_scale.shape[0]
    n_heads, chunk, _ = w_s.shape
    halo = max(B_WINDOWS)
    assert len(B_WINDOWS) == w_pool.shape[0] and n_heads * A_HEAD == d_a
    assert tm % chunk == 0 and seq % tm == 0 and halo % SUBLANES == 0 and halo <= tm
    kern = functools.partial(_even_mixer_kernel, tiles_per_seq=seq // tm, d_a=d_a)
    full = lambda *shape: pl.BlockSpec(shape, lambda i: (0,) * len(shape))
    return pl.pallas_call(
        kern,
        grid=(n // tm,),
        in_specs=[
            pl.BlockSpec((tm, d), lambda i: (i, 0)),
            full(d, 2 * d_a + d_b),
            full(1, d_a),
            full(1, d_a),
            full(*w_s.shape),
            full(chunk, n_heads),
            full(*w_pool.shape),
            full(1, d_b),
        ],
        out_specs=pl.BlockSpec((tm, d_a + d_b), lambda i: (i, 0)),
        out_shape=jax.ShapeDtypeStruct((n, d_a + d_b), BF16),
        scratch_shapes=[pltpu.VMEM((halo + tm, d_b), F32)],
        compiler_params=pltpu.CompilerParams(
            dimension_semantics=("arbitrary",), vmem_limit_bytes=VMEM_LIMIT_BYTES),
        name="even_mixer",
    )(x, w_in, ln_g.reshape(1, d_a), ln_b.reshape(1, d_a), w_s, b_s.T, w_pool,
      pool_scale.reshape(1, d_b))


def _split3(a):
    hi = a.astype(BF16)
    r = a - hi.astype(F32)
    mid = r.astype(BF16)
    lo = (r - mid.astype(F32)).astype(BF16)
    return hi, mid, lo


def _odd_mixer_kernel(x_ref, wq_ref, wf_ref, wi_ref, wg_ref, lbp_ref, ng_ref, y_ref,
                      xa_ref, st_ref, *, tiles_per_seq, layer):
    i = pl.program_id(0)
    j = pl.program_id(1)
    tm = x_ref.shape[0]
    hw = wq_ref.shape[1]
    heads = hw // C_HEAD

    @pl.when(j == 0)
    def _():
        xa_ref[...] = x_ref[...].astype(BF16)

    @pl.when(i % tiles_per_seq == 0)
    def _():
        for hd in range(heads):
            st_ref[j * heads + hd] = jnp.zeros((C_HEAD, C_HEAD), F32)

    xa = xa_ref[...]
    hq = _dot(xa, wq_ref[...])
    hf = _dot(xa, wf_ref[...])
    hv = _dot(xa, wi_ref[...])
    hg = _dot(xa, wg_ref[...])

    lbp = lbp_ref[...]
    e = jnp.exp(lbp - jnp.max(lbp, axis=0, keepdims=True))
    sm = e / jnp.sum(e, axis=0, keepdims=True)
    lb = jnp.zeros((1, hw), F32)
    for r in range(1, layer + 1):
        lb = lb + sm[r:r + 1, :]

    f = lb + (1.0 - lb) * jax.nn.sigmoid(hf)
    qq = jax.nn.silu(hq)
    kk = 1.0 - f
    logf = jnp.log(f)
    gate = jax.nn.sigmoid(hg)
    ng = ng_ref[...]

    row = lax.broadcasted_iota(jnp.int32, (C_CHUNK, C_CHUNK), 0)
    col = lax.broadcasted_iota(jnp.int32, (C_CHUNK, C_CHUNK), 1)
    causal = col <= row
    tri = causal.astype(BF16)

    for c in range(tm // C_CHUNK):
        rs = slice(c * C_CHUNK, (c + 1) * C_CHUNK)
        hi, mid, lo = _split3(logf[rs])
        bcum = _dot(tri, hi) + _dot(tri, mid) + _dot(tri, lo)
        blast = bcum[C_CHUNK - 1:C_CHUNK, :]
        q_dec = (qq[rs] * jnp.exp(bcum)).astype(BF16)
        k_dec = (kk[rs] * jnp.exp(-bcum)).astype(BF16)
        k_end = (kk[rs] * jnp.exp(blast - bcum)).astype(BF16)
        dec = jnp.exp(blast)
        vb = hv[rs].astype(BF16)
        for hd in range(heads):
            cs = slice(hd * C_HEAD, (hd + 1) * C_HEAD)
            sid = j * heads + hd
            st = st_ref[sid]
            att = jnp.where(causal, _dot_nt(q_dec[:, cs], k_dec[:, cs]), 0.0)
            o = _dot(att.astype(BF16), vb[:, cs]) + _dot_nt(q_dec[:, cs], st.astype(BF16))
            st_ref[sid] = dec[:, cs] * st + _dot_tn(vb[:, cs], k_end[:, cs])
            o = o * lax.rsqrt(jnp.mean(jnp.square(o), axis=-1, keepdims=True) + LN_EPS) * ng[:, cs]
            y_ref[rs, cs] = (o * gate[rs, cs]).astype(BF16)


def _odd_mixer(x, w_in, lb_param, norm_g, *, layer, seq, tm, hw):
    n, d = x.shape
    d_c = norm_g.shape[0]
    depth = lb_param.shape[0]
    nj = d_c // hw
    assert hw % C_HEAD == 0 and d_c % hw == 0 and tm % C_CHUNK == 0 and seq % tm == 0
    kern = functools.partial(_odd_mixer_kernel, tiles_per_seq=seq // tm, layer=layer)
    wspec = lambda part: pl.BlockSpec((d, hw), lambda i, j: (0, part * nj + j))
    return pl.pallas_call(
        kern,
        grid=(n // tm, nj),
        in_specs=[
            pl.BlockSpec((tm, d), lambda i, j: (i, 0)),
            wspec(0), wspec(1), wspec(2), wspec(3),
            pl.BlockSpec((depth, hw), lambda i, j: (0, j)),
            pl.BlockSpec((1, hw), lambda i, j: (0, j)),
        ],
        out_specs=pl.BlockSpec((tm, hw), lambda i, j: (i, j)),
        out_shape=jax.ShapeDtypeStruct((n, d_c), BF16),
        scratch_shapes=[
            pltpu.VMEM((tm, d), BF16),
            pltpu.VMEM((d_c // C_HEAD, C_HEAD, C_HEAD), F32),
        ],
        compiler_params=pltpu.CompilerParams(
            dimension_semantics=("arbitrary", "arbitrary"), vmem_limit_bytes=VMEM_LIMIT_BYTES),
        name="odd_mixer",
    )(x, w_in, w_in, w_in, w_in, lb_param, norm_g.reshape(1, d_c))


def kernel(x, ev_w_in, ev_ln_v_g, ev_ln_v_b, ev_w_s, ev_b_s, ev_w_pool, ev_pool_scale,
           ev_w_out, od_w_in, od_norm_g, od_w_out, lb_param, ffn_w_up, ffn_conv_w,
           ffn_conv_b, ffn_w_down, ln1_g, ln1_b, ln2_g, ln2_b):
    bn, seq, d = x.shape
    depth = ln1_g.shape[0]
    alpha = (2 * depth) ** 0.25
    xf = x.reshape(bn * seq, d)
    for l in range(depth):
        if l % 2 == 0:
            e = l // 2
            y = _even_mixer(xf, ev_w_in[e].astype(BF16), ev_ln_v_g[e], ev_ln_v_b[e], ev_w_s[e],
                            ev_b_s[e], ev_w_pool[e].astype(BF16), ev_pool_scale[e],
                            seq=seq, tm=min(256, seq))
            w_out = ev_w_out[e]
        else:
            o = l // 2
            y = _odd_mixer(xf, od_w_in[o].astype(BF16), lb_param, od_norm_g[o],
                           layer=l, seq=seq, tm=min(512, seq), hw=256)
            w_out = od_w_out[o]
        xf = _proj_ln(y, w_out.astype(BF16), xf, ln1_g[l], ln1_b[l], alpha=alpha, tm=min(512, seq))
        xf = _conv_ffn(xf, ffn_w_up[l].astype(BF16), ffn_conv_w[l], ffn_conv_b[l],
                       ffn_w_down[l].astype(BF16), ln2_g[l], ln2_b[l],
                       alpha=alpha, seq=seq, tm=min(512, seq), tf=512)
    return xf.reshape(bn, seq, d)
```

```python
import functools

import jax
import jax.numpy as jnp
from jax import lax
from jax.experimental import pallas as pl
from jax.experimental.pallas import tpu as pltpu

F32 = jnp.float32
BF16 = jnp.bfloat16

A_HEAD = 128
B_WINDOWS = (2, 4, 8, 16)
C_HEAD = 128
C_CHUNK = 64
LN_EPS = 1e-5

SUBLANES = 8
MXU_DIM = 256
VMEM_LIMIT_BYTES = 56 * 1024 * 1024


def _dot(a, b):
    return jnp.dot(a, b, preferred_element_type=F32)


def _dot_nt(a, b):
    return lax.dot_general(a, b, (((1,), (1,)), ((), ())), preferred_element_type=F32)


def _dot_tn(a, b):
    return lax.dot_general(a, b, (((0,), (0,)), ((), ())), preferred_element_type=F32)


def _layer_norm(z, g, b):
    mu = jnp.mean(z, axis=-1, keepdims=True)
    zc = z - mu
    var = jnp.mean(jnp.square(zc), axis=-1, keepdims=True)
    return zc * lax.rsqrt(var + LN_EPS) * g + b


def _proj_ln_kernel(y_ref, w_ref, x_ref, g_ref, b_ref, o_ref, ob_ref, *, alpha):
    mix = _dot(y_ref[...], w_ref[...])
    out = _layer_norm(alpha * x_ref[...] + mix, g_ref[...], b_ref[...])
    o_ref[...] = out
    ob_ref[...] = out.astype(BF16)


def _proj_ln(y, w, x, g, b, *, alpha, tm):
    n, d = x.shape
    k = y.shape[1]
    return pl.pallas_call(
        functools.partial(_proj_ln_kernel, alpha=alpha),
        grid=(n // tm,),
        in_specs=[
            pl.BlockSpec((tm, k), lambda i: (i, 0)),
            pl.BlockSpec((k, d), lambda i: (0, 0)),
            pl.BlockSpec((tm, d), lambda i: (i, 0)),
            pl.BlockSpec((1, d), lambda i: (0, 0)),
            pl.BlockSpec((1, d), lambda i: (0, 0)),
        ],
        out_specs=[pl.BlockSpec((tm, d), lambda i: (i, 0)), pl.BlockSpec((tm, d), lambda i: (i, 0))],
        out_shape=[jax.ShapeDtypeStruct((n, d), F32), jax.ShapeDtypeStruct((n, d), BF16)],
        compiler_params=pltpu.CompilerParams(
            dimension_semantics=("arbitrary",), vmem_limit_bytes=VMEM_LIMIT_BYTES),
        name="proj_ln",
    )(y, w, x, g.reshape(1, d), b.reshape(1, d))


def _conv_ffn_kernel(xb_ref, xr_ref, wa_ref, wv_ref, cwa_ref, cwv_ref, cba_ref, cbv_ref, wd_ref,
                     g_ref, b_ref, o_ref, acc_ref, h0_ref, h1_ref, g0_ref, g1_ref, carry_ref,
                     *, alpha, nf, tiles_per_seq, taps):
    s = pl.program_id(0)
    tm = xb_ref.shape[0]
    tf = wa_ref.shape[1]
    halo = SUBLANES
    i = s // nf
    j = s % nf

    @pl.when(s == 0)
    def _():
        h1_ref[...] = jnp.zeros(h1_ref.shape, F32)
        g0_ref[...] = jnp.zeros(g0_ref.shape, BF16)
        acc_ref[...] = jnp.zeros(acc_ref.shape, F32)
        carry_ref[...] = jnp.zeros(carry_ref.shape, F32)

    def stage(hw_ref, hr_ref, gw_ref, gr_ref):
        seq_start = (i % tiles_per_seq) == 0
        d_in = xb_ref.shape[1]
        kt = MXU_DIM
        n_k = d_in // kt
        rb = tm // (2 * n_k) // 2
        d_out = acc_ref.shape[1]

        def gate(blk):
            r0 = blk * rb

            def conv(cs, cw_ref, cb_ref):
                cw = cw_ref[...]
                hc = cb_ref[...] + cw[taps - 1:taps, :] * hr_ref[halo + r0:halo + r0 + rb, cs]
                for tap in range(taps - 1):
                    sh = taps - 1 - tap
                    hc = hc + cw[tap:tap + 1, :] * hr_ref[halo + r0 - sh:halo + r0 - sh + rb, cs]
                return hc

            a = conv(slice(0, tf), cwa_ref, cba_ref)
            v = conv(slice(tf, 2 * tf), cwv_ref, cbv_ref)
            gw_ref[r0:r0 + rb, :] = (jax.nn.silu(a) * v).astype(BF16)

        blk = 0
        for half, w_ref in enumerate((wa_ref, wv_ref)):
            cs = slice(half * tf, (half + 1) * tf)
            h = None
            for k in range(n_k):
                ks = slice(k * kt, (k + 1) * kt)
                part = _dot(xb_ref[:, ks], w_ref[ks, :])
                h = part if h is None else h + part
                gate(blk)
                gate(blk + 1)
                blk += 2
            hw_ref[0:halo, cs] = jnp.where(seq_start, 0.0, carry_ref[j, :, cs])
            hw_ref[halo:halo + tm, cs] = h
            carry_ref[j, :, cs] = h[tm - halo:, :]

        for n0 in range(0, d_out, kt):
            ns = slice(n0, n0 + kt)
            acc_ref[:, ns] += _dot(gr_ref[...], wd_ref[:, ns])

    @pl.when(s % 2 == 0)
    def _():
        stage(h0_ref, h1_ref, g1_ref, g0_ref)

    @pl.when(s % 2 == 1)
    def _():
        stage(h1_ref, h0_ref, g0_ref, g1_ref)

    @pl.when(s < 2)
    def _():
        acc_ref[...] = jnp.zeros(acc_ref.shape, F32)

    @pl.when((s >= 2) & ((s - 2) % nf == nf - 1))
    def _():
        o_ref[...] = _layer_norm(alpha * xr_ref[...] + acc_ref[...], g_ref[...], b_ref[...])
        acc_ref[...] = jnp.zeros(acc_ref.shape, F32)


def _conv_ffn(x, xb, w_up, conv_w, conv_b, w_down, g, b, *, alpha, seq, tm, tf):
    n, d = x.shape
    d_ff = w_down.shape[0]
    nf = d_ff // tf
    nm = n // tm
    taps = conv_w.shape[0]
    halo = SUBLANES
    assert taps - 1 <= halo and seq % tm == 0 and d_ff % tf == 0 and n % tm == 0
    kern = functools.partial(_conv_ffn_kernel, alpha=alpha, nf=nf, tiles_per_seq=seq // tm, taps=taps)
    last = nm * nf - 1
    up_tile = lambda s: jnp.minimum(s, last) // nf
    up_col = lambda s: jnp.minimum(s, last) % nf
    conv_col = lambda s: jnp.clip(s - 1, 0, last) % nf
    down_tile = lambda s: jnp.maximum(s - 2, 0) // nf
    down_col = lambda s: jnp.maximum(s - 2, 0) % nf
    cb2 = conv_b.reshape(1, -1)
    return pl.pallas_call(
        kern,
        grid=(nm * nf + 2,),
        in_specs=[
            pl.BlockSpec((tm, d), lambda s: (up_tile(s), 0)),
            pl.BlockSpec((tm, d), lambda s: (down_tile(s), 0)),
            pl.BlockSpec((d, tf), lambda s: (0, up_col(s))),
            pl.BlockSpec((d, tf), lambda s: (0, nf + up_col(s))),
            pl.BlockSpec((taps, tf), lambda s: (0, conv_col(s))),
            pl.BlockSpec((taps, tf), lambda s: (0, nf + conv_col(s))),
            pl.BlockSpec((1, tf), lambda s: (0, conv_col(s))),
            pl.BlockSpec((1, tf), lambda s: (0, nf + conv_col(s))),
            pl.BlockSpec((tf, d), lambda s: (down_col(s), 0)),
            pl.BlockSpec((1, d), lambda s: (0, 0)),
            pl.BlockSpec((1, d), lambda s: (0, 0)),
        ],
        out_specs=pl.BlockSpec((tm, d), lambda s: (down_tile(s), 0)),
        out_shape=jax.ShapeDtypeStruct((n, d), F32),
        scratch_shapes=[
            pltpu.VMEM((tm, d), F32),
            pltpu.VMEM((halo + tm, 2 * tf), F32),
            pltpu.VMEM((halo + tm, 2 * tf), F32),
            pltpu.VMEM((tm, tf), BF16),
            pltpu.VMEM((tm, tf), BF16),
            pltpu.VMEM((nf, halo, 2 * tf), F32),
        ],
        compiler_params=pltpu.CompilerParams(
            dimension_semantics=("arbitrary",), vmem_limit_bytes=VMEM_LIMIT_BYTES),
        name="conv_ffn",
    )(xb, x, w_up, w_up, conv_w, conv_w, cb2, cb2, w_down, g.reshape(1, d), b.reshape(1, d))


def _even_mixer_kernel(x_ref, win_ref, lng_ref, lnb_ref, ws_ref, bst_ref, wp_ref, ps_ref,
                       y_ref, ext_ref, *, tiles_per_seq, d_a):
    i = pl.program_id(0)
    tm = x_ref.shape[0]
    n_heads = ws_ref.shape[0]
    chunk = ws_ref.shape[1]
    n_groups = wp_ref.shape[0]
    grp = wp_ref.shape[1]
    halo = ext_ref.shape[0] - tm

    h = _dot(x_ref[...].astype(BF16), win_ref[...])
    za = jax.nn.gelu(h[:, :2 * d_a])
    u = za[:, :d_a]
    v = _layer_norm(za[:, d_a:], lng_ref[...], lnb_ref[...]).astype(BF16)

    row = lax.broadcasted_iota(jnp.int32, (chunk, chunk), 0)
    col = lax.broadcasted_iota(jnp.int32, (chunk, chunk), 1)
    causal = col <= row
    bst = bst_ref[...]
    for hd in range(n_heads):
        w = jnp.where(causal, ws_ref[hd], 0.0).astype(BF16)
        cs = slice(hd * A_HEAD, (hd + 1) * A_HEAD)
        for c in range(tm // chunk):
            rs = slice(c * chunk, (c + 1) * chunk)
            s = _dot(w, v[rs, cs]) + bst[:, hd:hd + 1]
            y_ref[rs, cs] = (u[rs, cs] * s).astype(BF16)

    xb = h[:, 2 * d_a:]

    @pl.when(i % tiles_per_seq == 0)
    def _():
        ext_ref[0:halo, :] = jnp.zeros((halo, xb.shape[1]), F32)

    @pl.when(i % tiles_per_seq != 0)
    def _():
        ext_ref[0:halo, :] = ext_ref[tm:tm + halo, :]

    ext_ref[halo:halo + tm, :] = xb
    pos = (i % tiles_per_seq) * tm + lax.broadcasted_iota(jnp.int32, (tm, 1), 0)
    posf = (pos + 1).astype(F32)
    for gi in range(n_groups):
        win = B_WINDOWS[gi]
        cs = slice(gi * grp, (gi + 1) * grp)
        wsum = xb[:, cs]
        for k in range(1, win):
            wsum = wsum + ext_ref[halo - k:halo - k + tm, cs]
        cnt = jnp.minimum(posf, float(win))
        p = wsum / cnt - xb[:, cs]
        yb = _dot(p.astype(BF16), wp_ref[gi]) * ps_ref[:, cs]
        y_ref[:, d_a + gi * grp:d_a + (gi + 1) * grp] = yb.astype(BF16)


def _even_mixer(x, w_in, ln_g, ln_b, w_s, b_s, w_pool, pool_scale, *, seq, tm):
    n, d = x.shape
    d_a = ln_g.shape[0]
    d_b = pool_scale.shape[0]
    n_heads, chunk, _ = w_s.shape
    halo = max(B_WINDOWS)
    assert len(B_WINDOWS) == w_pool.shape[0] and n_heads * A_HEAD == d_a
    assert tm % chunk == 0 and seq % tm == 0 and halo % SUBLANES == 0 and halo <= tm
    kern = functools.partial(_even_mixer_kernel, tiles_per_seq=seq // tm, d_a=d_a)
    full = lambda *shape: pl.BlockSpec(shape, lambda i: (0,) * len(shape))
    return pl.pallas_call(
        kern,
        grid=(n // tm,),
        in_specs=[
            pl.BlockSpec((tm, d), lambda i: (i, 0)),
            full(d, 2 * d_a + d_b),
            full(1, d_a),
            full(1, d_a),
            full(*w_s.shape),
            full(chunk, n_heads),
            full(*w_pool.shape),
            full(1, d_b),
        ],
        out_specs=pl.BlockSpec((tm, d_a + d_b), lambda i: (i, 0)),
        out_shape=jax.ShapeDtypeStruct((n, d_a + d_b), BF16),
        scratch_shapes=[pltpu.VMEM((halo + tm, d_b), F32)],
        compiler_params=pltpu.CompilerParams(
            dimension_semantics=("arbitrary",), vmem_limit_bytes=VMEM_LIMIT_BYTES),
        name="even_mixer",
    )(x, w_in, ln_g.reshape(1, d_a), ln_b.reshape(1, d_a), w_s, b_s.T, w_pool,
      pool_scale.reshape(1, d_b))


def _split3(a):
    hi = a.astype(BF16)
    r = a - hi.astype(F32)
    mid = r.astype(BF16)
    lo = (r - mid.astype(F32)).astype(BF16)
    return hi, mid, lo


def _odd_mixer_kernel(x_ref, wq_ref, wf_ref, wi_ref, wg_ref, lbp_ref, ng_ref, y_ref,
                      xa_ref, st_ref, *, tiles_per_seq, layer):
    i = pl.program_id(0)
    j = pl.program_id(1)
    tm = x_ref.shape[0]
    hw = wq_ref.shape[1]
    heads = hw // C_HEAD

    @pl.when(j == 0)
    def _():
        xa_ref[...] = x_ref[...].astype(BF16)

    @pl.when(i % tiles_per_seq == 0)
    def _():
        for hd in range(heads):
            st_ref[j * heads + hd] = jnp.zeros((C_HEAD, C_HEAD), F32)

    xa = xa_ref[...]
    hq = _dot(xa, wq_ref[...])
    hf = _dot(xa, wf_ref[...])
    hv = _dot(xa, wi_ref[...])
    hg = _dot(xa, wg_ref[...])

    lbp = lbp_ref[...]
    e = jnp.exp(lbp - jnp.max(lbp, axis=0, keepdims=True))
    sm = e / jnp.sum(e, axis=0, keepdims=True)
    lb = jnp.zeros((1, hw), F32)
    for r in range(1, layer + 1):
        lb = lb + sm[r:r + 1, :]

    f = lb + (1.0 - lb) * jax.nn.sigmoid(hf)
    qq = jax.nn.silu(hq)
    kk = 1.0 - f
    logf = jnp.log(f)
    gate = jax.nn.sigmoid(hg)
    ng = ng_ref[...]

    row = lax.broadcasted_iota(jnp.int32, (C_CHUNK, C_CHUNK), 0)
    col = lax.broadcasted_iota(jnp.int32, (C_CHUNK, C_CHUNK), 1)
    causal = col <= row
    tri = causal.astype(BF16)

    for c in range(tm // C_CHUNK):
        rs = slice(c * C_CHUNK, (c + 1) * C_CHUNK)
        hi, mid, lo = _split3(logf[rs])
        bcum = _dot(tri, hi) + _dot(tri, mid) + _dot(tri, lo)
        blast = bcum[C_CHUNK - 1:C_CHUNK, :]
        q_dec = (qq[rs] * jnp.exp(bcum)).astype(BF16)
        k_dec = (kk[rs] * jnp.exp(-bcum)).astype(BF16)
        k_end = (kk[rs] * jnp.exp(blast - bcum)).astype(BF16)
        dec = jnp.exp(blast)
        vb = hv[rs].astype(BF16)
        for hd in range(heads):
            cs = slice(hd * C_HEAD, (hd + 1) * C_HEAD)
            sid = j * heads + hd
            st = st_ref[sid]
            att = jnp.where(causal, _dot_nt(q_dec[:, cs], k_dec[:, cs]), 0.0)
            o = _dot(att.astype(BF16), vb[:, cs]) + _dot_nt(q_dec[:, cs], st.astype(BF16))
            st_ref[sid] = dec[:, cs] * st + _dot_tn(vb[:, cs], k_end[:, cs])
            o = o * lax.rsqrt(jnp.mean(jnp.square(o), axis=-1, keepdims=True) + LN_EPS) * ng[:, cs]
            y_ref[rs, cs] = (o * gate[rs, cs]).astype(BF16)


def _odd_mixer(x, w_in, lb_param, norm_g, *, layer, seq, tm, hw):
    n, d = x.shape
    d_c = norm_g.shape[0]
    depth = lb_param.shape[0]
    nj = d_c // hw
    assert hw % C_HEAD == 0 and d_c % hw == 0 and tm % C_CHUNK == 0 and seq % tm == 0
    kern = functools.partial(_odd_mixer_kernel, tiles_per_seq=seq // tm, layer=layer)
    wspec = lambda part: pl.BlockSpec((d, hw), lambda i, j: (0, part * nj + j))
    return pl.pallas_call(
        kern,
        grid=(n // tm, nj),
        in_specs=[
            pl.BlockSpec((tm, d), lambda i, j: (i, 0)),
            wspec(0), wspec(1), wspec(2), wspec(3),
            pl.BlockSpec((depth, hw), lambda i, j: (0, j)),
            pl.BlockSpec((1, hw), lambda i, j: (0, j)),
        ],
        out_specs=pl.BlockSpec((tm, hw), lambda i, j: (i, j)),
        out_shape=jax.ShapeDtypeStruct((n, d_c), BF16),
        scratch_shapes=[
            pltpu.VMEM((tm, d), BF16),
            pltpu.VMEM((d_c // C_HEAD, C_HEAD, C_HEAD), F32),
        ],
        compiler_params=pltpu.CompilerParams(
            dimension_semantics=("arbitrary", "arbitrary"), vmem_limit_bytes=VMEM_LIMIT_BYTES),
        name="odd_mixer",
    )(x, w_in, w_in, w_in, w_in, lb_param, norm_g.reshape(1, d_c))


def kernel(x, ev_w_in, ev_ln_v_g, ev_ln_v_b, ev_w_s, ev_b_s, ev_w_pool, ev_pool_scale,
           ev_w_out, od_w_in, od_norm_g, od_w_out, lb_param, ffn_w_up, ffn_conv_w,
           ffn_conv_b, ffn_w_down, ln1_g, ln1_b, ln2_g, ln2_b):
    bn, seq, d = x.shape
    depth = ln1_g.shape[0]
    alpha = (2 * depth) ** 0.25
    xf = x.reshape(bn * seq, d)
    for l in range(depth):
        if l % 2 == 0:
            e = l // 2
            y = _even_mixer(xf, ev_w_in[e].astype(BF16), ev_ln_v_g[e], ev_ln_v_b[e], ev_w_s[e],
                            ev_b_s[e], ev_w_pool[e].astype(BF16), ev_pool_scale[e],
                            seq=seq, tm=min(256, seq))
            w_out = ev_w_out[e]
        else:
            o = l // 2
            y = _odd_mixer(xf, od_w_in[o].astype(BF16), lb_param, od_norm_g[o],
                           layer=l, seq=seq, tm=min(512, seq), hw=256)
            w_out = od_w_out[o]
        xf, xb = _proj_ln(y, w_out.astype(BF16), xf, ln1_g[l], ln1_b[l], alpha=alpha, tm=min(512, seq))
        xf = _conv_ffn(xf, xb, ffn_w_up[l].astype(BF16), ffn_conv_w[l], ffn_conv_b[l],
                       ffn_w_down[l].astype(BF16), ln2_g[l], ln2_b[l],
                       alpha=alpha, seq=seq, tm=min(512, seq), tf=512)
    return xf.reshape(bn, seq, d)
```

```python
import functools

import jax
import jax.numpy as jnp
from jax import lax
from jax.experimental import pallas as pl
from jax.experimental.pallas import tpu as pltpu

F32 = jnp.float32
BF16 = jnp.bfloat16

A_HEAD = 128
B_WINDOWS = (2, 4, 8, 16)
C_HEAD = 128
C_CHUNK = 64
LN_EPS = 1e-5

SUBLANES = 8
MXU_DIM = 256
VMEM_LIMIT_BYTES = 56 * 1024 * 1024


def _dot(a, b):
    return jnp.dot(a, b, preferred_element_type=F32)


def _dot_nt(a, b):
    return lax.dot_general(a, b, (((1,), (1,)), ((), ())), preferred_element_type=F32)


def _dot_tn(a, b):
    return lax.dot_general(a, b, (((0,), (0,)), ((), ())), preferred_element_type=F32)


def _layer_norm(z, g, b):
    mu = jnp.mean(z, axis=-1, keepdims=True)
    zc = z - mu
    var = jnp.mean(jnp.square(zc), axis=-1, keepdims=True)
    return zc * lax.rsqrt(var + LN_EPS) * g + b


def _proj_ln_kernel(y_ref, w_ref, x_ref, g_ref, b_ref, o_ref, ob_ref, *, alpha):
    mix = _dot(y_ref[...], w_ref[...])
    out = _layer_norm(alpha * x_ref[...] + mix, g_ref[...], b_ref[...])
    o_ref[...] = out
    ob_ref[...] = out.astype(BF16)


def _proj_ln(y, w, x, g, b, *, alpha, tm):
    n, d = x.shape
    k = y.shape[1]
    return pl.pallas_call(
        functools.partial(_proj_ln_kernel, alpha=alpha),
        grid=(n // tm,),
        in_specs=[
            pl.BlockSpec((tm, k), lambda i: (i, 0)),
            pl.BlockSpec((k, d), lambda i: (0, 0)),
            pl.BlockSpec((tm, d), lambda i: (i, 0)),
            pl.BlockSpec((1, d), lambda i: (0, 0)),
            pl.BlockSpec((1, d), lambda i: (0, 0)),
        ],
        out_specs=[pl.BlockSpec((tm, d), lambda i: (i, 0)), pl.BlockSpec((tm, d), lambda i: (i, 0))],
        out_shape=[jax.ShapeDtypeStruct((n, d), F32), jax.ShapeDtypeStruct((n, d), BF16)],
        compiler_params=pltpu.CompilerParams(
            dimension_semantics=("arbitrary",), vmem_limit_bytes=VMEM_LIMIT_BYTES),
        name="proj_ln",
    )(y, w, x, g.reshape(1, d), b.reshape(1, d))


def _conv_ffn_kernel(xb_ref, xr_ref, wa_ref, wv_ref, cwa_ref, cwv_ref, cba_ref, cbv_ref, wd_ref,
                     g_ref, b_ref, o_ref, acc_ref, h0_ref, h1_ref, g0_ref, g1_ref, carry_ref,
                     *, alpha, nf, tiles_per_seq, taps):
    s = pl.program_id(0)
    tm = xb_ref.shape[0]
    tf = wa_ref.shape[1]
    halo = SUBLANES
    i = s // nf
    j = s % nf

    @pl.when(s == 0)
    def _():
        h1_ref[...] = jnp.zeros(h1_ref.shape, F32)
        g0_ref[...] = jnp.zeros(g0_ref.shape, BF16)
        acc_ref[...] = jnp.zeros(acc_ref.shape, F32)
        carry_ref[...] = jnp.zeros(carry_ref.shape, F32)

    def stage(hw_ref, hr_ref, gw_ref, gr_ref):
        seq_start = (i % tiles_per_seq) == 0
        d_in = xb_ref.shape[1]
        kt = MXU_DIM
        n_k = d_in // kt
        rb = tm // (2 * n_k) // 2
        d_out = acc_ref.shape[1]

        def gate(blk):
            r0 = blk * rb

            def conv(cs, cw_ref, cb_ref):
                cw = cw_ref[...]
                hc = cb_ref[...] + cw[taps - 1:taps, :] * hr_ref[halo + r0:halo + r0 + rb, cs]
                for tap in range(taps - 1):
                    sh = taps - 1 - tap
                    hc = hc + cw[tap:tap + 1, :] * hr_ref[halo + r0 - sh:halo + r0 - sh + rb, cs]
                return hc

            a = conv(slice(0, tf), cwa_ref, cba_ref)
            v = conv(slice(tf, 2 * tf), cwv_ref, cbv_ref)
            gw_ref[r0:r0 + rb, :] = (jax.nn.silu(a) * v).astype(BF16)

        blk = 0
        for half, w_ref in enumerate((wa_ref, wv_ref)):
            cs = slice(half * tf, (half + 1) * tf)
            h = None
            for k in range(n_k):
                ks = slice(k * kt, (k + 1) * kt)
                part = _dot(xb_ref[:, ks], w_ref[ks, :])
                h = part if h is None else h + part
                gate(blk)
                gate(blk + 1)
                blk += 2
            hw_ref[0:halo, cs] = jnp.where(seq_start, 0.0, carry_ref[j, :, cs])
            hw_ref[halo:halo + tm, cs] = h
            carry_ref[j, :, cs] = h[tm - halo:, :]

        for n0 in range(0, d_out, kt):
            ns = slice(n0, n0 + kt)
            acc_ref[:, ns] += _dot(gr_ref[...], wd_ref[:, ns])

    @pl.when(s % 2 == 0)
    def _():
        stage(h0_ref, h1_ref, g1_ref, g0_ref)

    @pl.when(s % 2 == 1)
    def _():
        stage(h1_ref, h0_ref, g0_ref, g1_ref)

    @pl.when(s < 2)
    def _():
        acc_ref[...] = jnp.zeros(acc_ref.shape, F32)

    @pl.when((s >= 2) & ((s - 2) % nf == nf - 1))
    def _():
        o_ref[...] = _layer_norm(alpha * xr_ref[...] + acc_ref[...], g_ref[...], b_ref[...])
        acc_ref[...] = jnp.zeros(acc_ref.shape, F32)


def _conv_ffn(x, xb, w_up, conv_w, conv_b, w_down, g, b, *, layer, alpha, seq, tm, tf):
    n, d = x.shape
    d_ff = w_down.shape[1]
    nf = d_ff // tf
    nm = n // tm
    taps = conv_w.shape[1]
    halo = SUBLANES
    assert taps - 1 <= halo and seq % tm == 0 and d_ff % tf == 0 and n % tm == 0
    kern = functools.partial(_conv_ffn_kernel, alpha=alpha, nf=nf, tiles_per_seq=seq // tm, taps=taps)
    last = nm * nf - 1
    up_tile = lambda s: jnp.minimum(s, last) // nf
    up_col = lambda s: jnp.minimum(s, last) % nf
    conv_col = lambda s: jnp.clip(s - 1, 0, last) % nf
    down_tile = lambda s: jnp.maximum(s - 2, 0) // nf
    down_col = lambda s: jnp.maximum(s - 2, 0) % nf
    cb3 = conv_b.reshape(conv_b.shape[0], 1, -1)
    return pl.pallas_call(
        kern,
        grid=(nm * nf + 2,),
        in_specs=[
            pl.BlockSpec((tm, d), lambda s: (up_tile(s), 0)),
            pl.BlockSpec((tm, d), lambda s: (down_tile(s), 0)),
            pl.BlockSpec((None, d, tf), lambda s: (layer, 0, up_col(s))),
            pl.BlockSpec((None, d, tf), lambda s: (layer, 0, nf + up_col(s))),
            pl.BlockSpec((None, taps, tf), lambda s: (layer, 0, conv_col(s))),
            pl.BlockSpec((None, taps, tf), lambda s: (layer, 0, nf + conv_col(s))),
            pl.BlockSpec((None, 1, tf), lambda s: (layer, 0, conv_col(s))),
            pl.BlockSpec((None, 1, tf), lambda s: (layer, 0, nf + conv_col(s))),
            pl.BlockSpec((None, tf, d), lambda s: (layer, down_col(s), 0)),
            pl.BlockSpec((1, d), lambda s: (0, 0)),
            pl.BlockSpec((1, d), lambda s: (0, 0)),
        ],
        out_specs=pl.BlockSpec((tm, d), lambda s: (down_tile(s), 0)),
        out_shape=jax.ShapeDtypeStruct((n, d), F32),
        scratch_shapes=[
            pltpu.VMEM((tm, d), F32),
            pltpu.VMEM((halo + tm, 2 * tf), F32),
            pltpu.VMEM((halo + tm, 2 * tf), F32),
            pltpu.VMEM((tm, tf), BF16),
            pltpu.VMEM((tm, tf), BF16),
            pltpu.VMEM((nf, halo, 2 * tf), F32),
        ],
        compiler_params=pltpu.CompilerParams(
            dimension_semantics=("arbitrary",), vmem_limit_bytes=VMEM_LIMIT_BYTES),
        name="conv_ffn",
    )(xb, x, w_up, w_up, conv_w, conv_w, cb3, cb3, w_down, g.reshape(1, d), b.reshape(1, d))


def _even_mixer_kernel(x_ref, win_ref, lng_ref, lnb_ref, ws_ref, bst_ref, wp_ref, ps_ref,
                       y_ref, ext_ref, *, tiles_per_seq, d_a):
    i = pl.program_id(0)
    tm = x_ref.shape[0]
    n_heads = ws_ref.shape[0]
    chunk = ws_ref.shape[1]
    n_groups = wp_ref.shape[0]
    grp = wp_ref.shape[1]
    halo = ext_ref.shape[0] - tm

    h = _dot(x_ref[...].astype(BF16), win_ref[...])
    za = jax.nn.gelu(h[:, :2 * d_a])
    u = za[:, :d_a]
    v = _layer_norm(za[:, d_a:], lng_ref[...], lnb_ref[...]).astype(BF16)

    row = lax.broadcasted_iota(jnp.int32, (chunk, chunk), 0)
    col = lax.broadcasted_iota(jnp.int32, (chunk, chunk), 1)
    causal = col <= row
    bst = bst_ref[...]
    for hd in range(n_heads):
        w = jnp.where(causal, ws_ref[hd], 0.0).astype(BF16)
        cs = slice(hd * A_HEAD, (hd + 1) * A_HEAD)
        for c in range(tm // chunk):
            rs = slice(c * chunk, (c + 1) * chunk)
            s = _dot(w, v[rs, cs]) + bst[:, hd:hd + 1]
            y_ref[rs, cs] = (u[rs, cs] * s).astype(BF16)

    xb = h[:, 2 * d_a:]

    @pl.when(i % tiles_per_seq == 0)
    def _():
        ext_ref[0:halo, :] = jnp.zeros((halo, xb.shape[1]), F32)

    @pl.when(i % tiles_per_seq != 0)
    def _():
        ext_ref[0:halo, :] = ext_ref[tm:tm + halo, :]

    ext_ref[halo:halo + tm, :] = xb
    pos = (i % tiles_per_seq) * tm + lax.broadcasted_iota(jnp.int32, (tm, 1), 0)
    posf = (pos + 1).astype(F32)
    for gi in range(n_groups):
        win = B_WINDOWS[gi]
        cs = slice(gi * grp, (gi + 1) * grp)
        wsum = xb[:, cs]
        for k in range(1, win):
            wsum = wsum + ext_ref[halo - k:halo - k + tm, cs]
        cnt = jnp.minimum(posf, float(win))
        p = wsum / cnt - xb[:, cs]
        yb = _dot(p.astype(BF16), wp_ref[gi]) * ps_ref[:, cs]
        y_ref[:, d_a + gi * grp:d_a + (gi + 1) * grp] = yb.astype(BF16)


def _even_mixer(x, w_in, ln_g, ln_b, w_s, b_s, w_pool, pool_scale, *, seq, tm):
    n, d = x.shape
    d_a = ln_g.shape[0]
    d_b = pool_scale.shape[0]
    n_heads, chunk, _ = w_s.shape
    halo = max(B_WINDOWS)
    assert len(B_WINDOWS) == w_pool.shape[0] and n_heads * A_HEAD == d_a
    assert tm % chunk == 0 and seq % tm == 0 and halo % SUBLANES == 0 and halo <= tm
    kern = functools.partial(_even_mixer_kernel, tiles_per_seq=seq // tm, d_a=d_a)
    full = lambda *shape: pl.BlockSpec(shape, lambda i: (0,) * len(shape))
    return pl.pallas_call(
        kern,
        grid=(n // tm,),
        in_specs=[
            pl.BlockSpec((tm, d), lambda i: (i, 0)),
            full(d, 2 * d_a + d_b),
            full(1, d_a),
            full(1, d_a),
            full(*w_s.shape),
            full(chunk, n_heads),
            full(*w_pool.shape),
            full(1, d_b),
        ],
        out_specs=pl.BlockSpec((tm, d_a + d_b), lambda i: (i, 0)),
        out_shape=jax.ShapeDtypeStruct((n, d_a + d_b), BF16),
        scratch_shapes=[pltpu.VMEM((halo + tm, d_b), F32)],
        compiler_params=pltpu.CompilerParams(
            dimension_semantics=("arbitrary",), vmem_limit_bytes=VMEM_LIMIT_BYTES),
        name="even_mixer",
    )(x, w_in, ln_g.reshape(1, d_a), ln_b.reshape(1, d_a), w_s, b_s.T, w_pool,
      pool_scale.reshape(1, d_b))


def _split3(a):
    hi = a.astype(BF16)
    r = a - hi.astype(F32)
    mid = r.astype(BF16)
    lo = (r - mid.astype(F32)).astype(BF16)
    return hi, mid, lo


def _odd_mixer_kernel(x_ref, wq_ref, wf_ref, wi_ref, wg_ref, lbp_ref, ng_ref, y_ref,
                      xa_ref, st_ref, *, tiles_per_seq, layer):
    i = pl.program_id(0)
    j = pl.program_id(1)
    tm = x_ref.shape[0]
    hw = wq_ref.shape[1]
    heads = hw // C_HEAD

    @pl.when(j == 0)
    def _():
        xa_ref[...] = x_ref[...].astype(BF16)

    @pl.when(i % tiles_per_seq == 0)
    def _():
        for hd in range(heads):
            st_ref[j * heads + hd] = jnp.zeros((C_HEAD, C_HEAD), F32)

    xa = xa_ref[...]
    hq = _dot(xa, wq_ref[...])
    hf = _dot(xa, wf_ref[...])
    hv = _dot(xa, wi_ref[...])
    hg = _dot(xa, wg_ref[...])

    lbp = lbp_ref[...]
    e = jnp.exp(lbp - jnp.max(lbp, axis=0, keepdims=True))
    sm = e / jnp.sum(e, axis=0, keepdims=True)
    lb = jnp.zeros((1, hw), F32)
    for r in range(1, layer + 1):
        lb = lb + sm[r:r + 1, :]

    f = lb + (1.0 - lb) * jax.nn.sigmoid(hf)
    qq = jax.nn.silu(hq)
    kk = 1.0 - f
    logf = jnp.log(f)
    gate = jax.nn.sigmoid(hg)
    ng = ng_ref[...]

    row = lax.broadcasted_iota(jnp.int32, (C_CHUNK, C_CHUNK), 0)
    col = lax.broadcasted_iota(jnp.int32, (C_CHUNK, C_CHUNK), 1)
    causal = col <= row
    tri = causal.astype(BF16)

    for c in range(tm // C_CHUNK):
        rs = slice(c * C_CHUNK, (c + 1) * C_CHUNK)
        hi, mid, lo = _split3(logf[rs])
        bcum = _dot(tri, hi) + _dot(tri, mid) + _dot(tri, lo)
        blast = bcum[C_CHUNK - 1:C_CHUNK, :]
        q_dec = (qq[rs] * jnp.exp(bcum)).astype(BF16)
        k_dec = (kk[rs] * jnp.exp(-bcum)).astype(BF16)
        k_end = (kk[rs] * jnp.exp(blast - bcum)).astype(BF16)
        dec = jnp.exp(blast)
        vb = hv[rs].astype(BF16)
        for hd in range(heads):
            cs = slice(hd * C_HEAD, (hd + 1) * C_HEAD)
            sid = j * heads + hd
            st = st_ref[sid]
            att = jnp.where(causal, _dot_nt(q_dec[:, cs], k_dec[:, cs]), 0.0)
            o = _dot(att.astype(BF16), vb[:, cs]) + _dot_nt(q_dec[:, cs], st.astype(BF16))
            st_ref[sid] = dec[:, cs] * st + _dot_tn(vb[:, cs], k_end[:, cs])
            o = o * lax.rsqrt(jnp.mean(jnp.square(o), axis=-1, keepdims=True) + LN_EPS) * ng[:, cs]
            y_ref[rs, cs] = (o * gate[rs, cs]).astype(BF16)


def _odd_mixer(x, w_in, lb_param, norm_g, *, layer, seq, tm, hw):
    n, d = x.shape
    d_c = norm_g.shape[0]
    depth = lb_param.shape[0]
    nj = d_c // hw
    assert hw % C_HEAD == 0 and d_c % hw == 0 and tm % C_CHUNK == 0 and seq % tm == 0
    kern = functools.partial(_odd_mixer_kernel, tiles_per_seq=seq // tm, layer=layer)
    wspec = lambda part: pl.BlockSpec((d, hw), lambda i, j: (0, part * nj + j))
    return pl.pallas_call(
        kern,
        grid=(n // tm, nj),
        in_specs=[
            pl.BlockSpec((tm, d), lambda i, j: (i, 0)),
            wspec(0), wspec(1), wspec(2), wspec(3),
            pl.BlockSpec((depth, hw), lambda i, j: (0, j)),
            pl.BlockSpec((1, hw), lambda i, j: (0, j)),
        ],
        out_specs=pl.BlockSpec((tm, hw), lambda i, j: (i, j)),
        out_shape=jax.ShapeDtypeStruct((n, d_c), BF16),
        scratch_shapes=[
            pltpu.VMEM((tm, d), BF16),
            pltpu.VMEM((d_c // C_HEAD, C_HEAD, C_HEAD), F32),
        ],
        compiler_params=pltpu.CompilerParams(
            dimension_semantics=("arbitrary", "arbitrary"), vmem_limit_bytes=VMEM_LIMIT_BYTES),
        name="odd_mixer",
    )(x, w_in, w_in, w_in, w_in, lb_param, norm_g.reshape(1, d_c))


def kernel(x, ev_w_in, ev_ln_v_g, ev_ln_v_b, ev_w_s, ev_b_s, ev_w_pool, ev_pool_scale,
           ev_w_out, od_w_in, od_norm_g, od_w_out, lb_param, ffn_w_up, ffn_conv_w,
           ffn_conv_b, ffn_w_down, ln1_g, ln1_b, ln2_g, ln2_b):
    bn, seq, d = x.shape
    depth = ln1_g.shape[0]
    alpha = (2 * depth) ** 0.25
    xf = x.reshape(bn * seq, d)
    ffn_w_up_b = ffn_w_up.astype(BF16)
    ffn_w_down_b = ffn_w_down.astype(BF16)
    for l in range(depth):
        if l % 2 == 0:
            e = l // 2
            y = _even_mixer(xf, ev_w_in[e].astype(BF16), ev_ln_v_g[e], ev_ln_v_b[e], ev_w_s[e],
                            ev_b_s[e], ev_w_pool[e].astype(BF16), ev_pool_scale[e],
                            seq=seq, tm=min(256, seq))
            w_out = ev_w_out[e]
        else:
            o = l // 2
            y = _odd_mixer(xf, od_w_in[o].astype(BF16), lb_param, od_norm_g[o],
                           layer=l, seq=seq, tm=min(512, seq), hw=256)
            w_out = od_w_out[o]
        xf, xb = _proj_ln(y, w_out.astype(BF16), xf, ln1_g[l], ln1_b[l], alpha=alpha, tm=min(512, seq))
        xf = _conv_ffn(xf, xb, ffn_w_up_b, ffn_conv_w, ffn_conv_b, ffn_w_down_b, ln2_g[l], ln2_b[l],
                       layer=l, alpha=alpha, seq=seq, tm=min(512, seq), tf=512)
    return xf.reshape(bn, seq, d)
```

```python
import functools

import jax
import jax.numpy as jnp
from jax import lax
from jax.experimental import pallas as pl
from jax.experimental.pallas import tpu as pltpu

F32 = jnp.float32
BF16 = jnp.bfloat16

A_HEAD = 128
B_WINDOWS = (2, 4, 8, 16)
C_HEAD = 128
C_CHUNK = 64
LN_EPS = 1e-5

SUBLANES = 8
MXU_DIM = 256
VMEM_LIMIT_BYTES = 56 * 1024 * 1024


def _dot(a, b):
    return jnp.dot(a, b, preferred_element_type=F32)


def _dot_nt(a, b):
    return lax.dot_general(a, b, (((1,), (1,)), ((), ())), preferred_element_type=F32)


def _dot_tn(a, b):
    return lax.dot_general(a, b, (((0,), (0,)), ((), ())), preferred_element_type=F32)


def _layer_norm(z, g, b):
    mu = jnp.mean(z, axis=-1, keepdims=True)
    zc = z - mu
    var = jnp.mean(jnp.square(zc), axis=-1, keepdims=True)
    return zc * lax.rsqrt(var + LN_EPS) * g + b


def _proj_ln_kernel(y_ref, w_ref, x_ref, g_ref, b_ref, o_ref, ob_ref, *, alpha):
    mix = _dot(y_ref[...], w_ref[...])
    out = _layer_norm(alpha * x_ref[...] + mix, g_ref[...], b_ref[...])
    o_ref[...] = out
    ob_ref[...] = out.astype(BF16)


def _proj_ln(y, w, x, g, b, *, alpha, tm):
    n, d = x.shape
    k = y.shape[1]
    return pl.pallas_call(
        functools.partial(_proj_ln_kernel, alpha=alpha),
        grid=(n // tm,),
        in_specs=[
            pl.BlockSpec((tm, k), lambda i: (i, 0)),
            pl.BlockSpec((k, d), lambda i: (0, 0)),
            pl.BlockSpec((tm, d), lambda i: (i, 0)),
            pl.BlockSpec((1, d), lambda i: (0, 0)),
            pl.BlockSpec((1, d), lambda i: (0, 0)),
        ],
        out_specs=[pl.BlockSpec((tm, d), lambda i: (i, 0)), pl.BlockSpec((tm, d), lambda i: (i, 0))],
        out_shape=[jax.ShapeDtypeStruct((n, d), F32), jax.ShapeDtypeStruct((n, d), BF16)],
        compiler_params=pltpu.CompilerParams(
            dimension_semantics=("arbitrary",), vmem_limit_bytes=VMEM_LIMIT_BYTES),
        name="proj_ln",
    )(y, w, x, g.reshape(1, d), b.reshape(1, d))


def _conv_ffn_kernel(xb_ref, xr_ref, wa_ref, wv_ref, cwa_ref, cwv_ref, cba_ref, cbv_ref, wd_ref,
                     g_ref, b_ref, o_ref, ob_ref, acc_ref, h0_ref, h1_ref, g0_ref, g1_ref, carry_ref,
                     *, alpha, nf, tiles_per_seq, taps):
    s = pl.program_id(0)
    tm = xb_ref.shape[0]
    tf = wa_ref.shape[1]
    halo = SUBLANES
    i = s // nf
    j = s % nf

    @pl.when(s == 0)
    def _():
        h1_ref[...] = jnp.zeros(h1_ref.shape, F32)
        g0_ref[...] = jnp.zeros(g0_ref.shape, BF16)
        acc_ref[...] = jnp.zeros(acc_ref.shape, F32)
        carry_ref[...] = jnp.zeros(carry_ref.shape, F32)

    def stage(hw_ref, hr_ref, gw_ref, gr_ref):
        seq_start = (i % tiles_per_seq) == 0
        d_in = xb_ref.shape[1]
        kt = MXU_DIM
        n_k = d_in // kt
        rb = tm // (2 * n_k) // 2
        d_out = acc_ref.shape[1]

        def gate(blk):
            r0 = blk * rb

            def conv(cs, cw_ref, cb_ref):
                cw = cw_ref[...]
                hc = cb_ref[...] + cw[taps - 1:taps, :] * hr_ref[halo + r0:halo + r0 + rb, cs]
                for tap in range(taps - 1):
                    sh = taps - 1 - tap
                    hc = hc + cw[tap:tap + 1, :] * hr_ref[halo + r0 - sh:halo + r0 - sh + rb, cs]
                return hc

            a = conv(slice(0, tf), cwa_ref, cba_ref)
            v = conv(slice(tf, 2 * tf), cwv_ref, cbv_ref)
            gw_ref[r0:r0 + rb, :] = (jax.nn.silu(a) * v).astype(BF16)

        blk = 0
        for half, w_ref in enumerate((wa_ref, wv_ref)):
            cs = slice(half * tf, (half + 1) * tf)
            h = None
            for k in range(n_k):
                ks = slice(k * kt, (k + 1) * kt)
                part = _dot(xb_ref[:, ks], w_ref[ks, :])
                h = part if h is None else h + part
                gate(blk)
                gate(blk + 1)
                blk += 2
            hw_ref[0:halo, cs] = jnp.where(seq_start, 0.0, carry_ref[j, :, cs])
            hw_ref[halo:halo + tm, cs] = h
            carry_ref[j, :, cs] = h[tm - halo:, :]

        for n0 in range(0, d_out, kt):
            ns = slice(n0, n0 + kt)
            acc_ref[:, ns] += _dot(gr_ref[...], wd_ref[:, ns])

    @pl.when(s % 2 == 0)
    def _():
        stage(h0_ref, h1_ref, g1_ref, g0_ref)

    @pl.when(s % 2 == 1)
    def _():
        stage(h1_ref, h0_ref, g0_ref, g1_ref)

    @pl.when(s < 2)
    def _():
        acc_ref[...] = jnp.zeros(acc_ref.shape, F32)

    @pl.when((s >= 2) & ((s - 2) % nf == nf - 1))
    def _():
        out = _layer_norm(alpha * xr_ref[...] + acc_ref[...], g_ref[...], b_ref[...])
        o_ref[...] = out
        ob_ref[...] = out.astype(BF16)
        acc_ref[...] = jnp.zeros(acc_ref.shape, F32)


def _conv_ffn(x, xb, w_up, conv_w, conv_b, w_down, g, b, *, layer, alpha, seq, tm, tf):
    n, d = x.shape
    d_ff = w_down.shape[1]
    nf = d_ff // tf
    nm = n // tm
    taps = conv_w.shape[1]
    halo = SUBLANES
    assert taps - 1 <= halo and seq % tm == 0 and d_ff % tf == 0 and n % tm == 0
    kern = functools.partial(_conv_ffn_kernel, alpha=alpha, nf=nf, tiles_per_seq=seq // tm, taps=taps)
    last = nm * nf - 1
    up_tile = lambda s: jnp.minimum(s, last) // nf
    up_col = lambda s: jnp.minimum(s, last) % nf
    conv_col = lambda s: jnp.clip(s - 1, 0, last) % nf
    down_tile = lambda s: jnp.maximum(s - 2, 0) // nf
    down_col = lambda s: jnp.maximum(s - 2, 0) % nf
    cb3 = conv_b.reshape(conv_b.shape[0], 1, -1)
    return pl.pallas_call(
        kern,
        grid=(nm * nf + 2,),
        in_specs=[
            pl.BlockSpec((tm, d), lambda s: (up_tile(s), 0)),
            pl.BlockSpec((tm, d), lambda s: (down_tile(s), 0)),
            pl.BlockSpec((None, d, tf), lambda s: (layer, 0, up_col(s))),
            pl.BlockSpec((None, d, tf), lambda s: (layer, 0, nf + up_col(s))),
            pl.BlockSpec((None, taps, tf), lambda s: (layer, 0, conv_col(s))),
            pl.BlockSpec((None, taps, tf), lambda s: (layer, 0, nf + conv_col(s))),
            pl.BlockSpec((None, 1, tf), lambda s: (layer, 0, conv_col(s))),
            pl.BlockSpec((None, 1, tf), lambda s: (layer, 0, nf + conv_col(s))),
            pl.BlockSpec((None, tf, d), lambda s: (layer, down_col(s), 0)),
            pl.BlockSpec((1, d), lambda s: (0, 0)),
            pl.BlockSpec((1, d), lambda s: (0, 0)),
        ],
        out_specs=[pl.BlockSpec((tm, d), lambda s: (down_tile(s), 0)),
                   pl.BlockSpec((tm, d), lambda s: (down_tile(s), 0))],
        out_shape=[jax.ShapeDtypeStruct((n, d), F32), jax.ShapeDtypeStruct((n, d), BF16)],
        scratch_shapes=[
            pltpu.VMEM((tm, d), F32),
            pltpu.VMEM((halo + tm, 2 * tf), F32),
            pltpu.VMEM((halo + tm, 2 * tf), F32),
            pltpu.VMEM((tm, tf), BF16),
            pltpu.VMEM((tm, tf), BF16),
            pltpu.VMEM((nf, halo, 2 * tf), F32),
        ],
        compiler_params=pltpu.CompilerParams(
            dimension_semantics=("arbitrary",), vmem_limit_bytes=VMEM_LIMIT_BYTES),
        name="conv_ffn",
    )(xb, x, w_up, w_up, conv_w, conv_w, cb3, cb3, w_down, g.reshape(1, d), b.reshape(1, d))


def _even_mixer_kernel(x_ref, win_ref, lng_ref, lnb_ref, ws_ref, bst_ref, wp_ref, ps_ref,
                       y_ref, ext_ref, *, tiles_per_seq, d_a):
    i = pl.program_id(0)
    tm = x_ref.shape[0]
    n_heads = ws_ref.shape[0]
    chunk = ws_ref.shape[1]
    n_groups = wp_ref.shape[0]
    grp = wp_ref.shape[1]
    halo = ext_ref.shape[0] - tm

    h = _dot(x_ref[...].astype(BF16), win_ref[...])
    za = jax.nn.gelu(h[:, :2 * d_a])
    u = za[:, :d_a]
    v = _layer_norm(za[:, d_a:], lng_ref[...], lnb_ref[...]).astype(BF16)

    row = lax.broadcasted_iota(jnp.int32, (chunk, chunk), 0)
    col = lax.broadcasted_iota(jnp.int32, (chunk, chunk), 1)
    causal = col <= row
    bst = bst_ref[...]
    for hd in range(n_heads):
        w = jnp.where(causal, ws_ref[hd], 0.0).astype(BF16)
        cs = slice(hd * A_HEAD, (hd + 1) * A_HEAD)
        for c in range(tm // chunk):
            rs = slice(c * chunk, (c + 1) * chunk)
            s = _dot(w, v[rs, cs]) + bst[:, hd:hd + 1]
            y_ref[rs, cs] = (u[rs, cs] * s).astype(BF16)

    xb = h[:, 2 * d_a:]

    @pl.when(i % tiles_per_seq == 0)
    def _():
        ext_ref[0:halo, :] = jnp.zeros((halo, xb.shape[1]), F32)

    @pl.when(i % tiles_per_seq != 0)
    def _():
        ext_ref[0:halo, :] = ext_ref[tm:tm + halo, :]

    ext_ref[halo:halo + tm, :] = xb
    pos = (i % tiles_per_seq) * tm + lax.broadcasted_iota(jnp.int32, (tm, 1), 0)
    posf = (pos + 1).astype(F32)
    for gi in range(n_groups):
        win = B_WINDOWS[gi]
        cs = slice(gi * grp, (gi + 1) * grp)
        wsum = xb[:, cs]
        for k in range(1, win):
            wsum = wsum + ext_ref[halo - k:halo - k + tm, cs]
        cnt = jnp.minimum(posf, float(win))
        p = wsum / cnt - xb[:, cs]
        yb = _dot(p.astype(BF16), wp_ref[gi]) * ps_ref[:, cs]
        y_ref[:, d_a + gi * grp:d_a + (gi + 1) * grp] = yb.astype(BF16)


def _even_mixer(x, w_in, ln_g, ln_b, w_s, b_s, w_pool, pool_scale, *, seq, tm):
    n, d = x.shape
    d_a = ln_g.shape[0]
    d_b = pool_scale.shape[0]
    n_heads, chunk, _ = w_s.shape
    halo = max(B_WINDOWS)
    assert len(B_WINDOWS) == w_pool.shape[0] and n_heads * A_HEAD == d_a
    assert tm % chunk == 0 and seq % tm == 0 and halo % SUBLANES == 0 and halo <= tm
    kern = functools.partial(_even_mixer_kernel, tiles_per_seq=seq // tm, d_a=d_a)
    full = lambda *shape: pl.BlockSpec(shape, lambda i: (0,) * len(shape))
    return pl.pallas_call(
        kern,
        grid=(n // tm,),
        in_specs=[
            pl.BlockSpec((tm, d), lambda i: (i, 0)),
            full(d, 2 * d_a + d_b),
            full(1, d_a),
            full(1, d_a),
            full(*w_s.shape),
            full(chunk, n_heads),
            full(*w_pool.shape),
            full(1, d_b),
        ],
        out_specs=pl.BlockSpec((tm, d_a + d_b), lambda i: (i, 0)),
        out_shape=jax.ShapeDtypeStruct((n, d_a + d_b), BF16),
        scratch_shapes=[pltpu.VMEM((halo + tm, d_b), F32)],
        compiler_params=pltpu.CompilerParams(
            dimension_semantics=("arbitrary",), vmem_limit_bytes=VMEM_LIMIT_BYTES),
        name="even_mixer",
    )(x, w_in, ln_g.reshape(1, d_a), ln_b.reshape(1, d_a), w_s, b_s.T, w_pool,
      pool_scale.reshape(1, d_b))


def _split3(a):
    hi = a.astype(BF16)
    r = a - hi.astype(F32)
    mid = r.astype(BF16)
    lo = (r - mid.astype(F32)).astype(BF16)
    return hi, mid, lo


def _odd_mixer_kernel(x_ref, wq_ref, wf_ref, wi_ref, wg_ref, lbp_ref, ng_ref, y_ref,
                      xa_ref, st_ref, *, tiles_per_seq, layer):
    i = pl.program_id(0)
    j = pl.program_id(1)
    tm = x_ref.shape[0]
    hw = wq_ref.shape[1]
    heads = hw // C_HEAD

    @pl.when(j == 0)
    def _():
        xa_ref[...] = x_ref[...].astype(BF16)

    @pl.when(i % tiles_per_seq == 0)
    def _():
        for hd in range(heads):
            st_ref[j * heads + hd] = jnp.zeros((C_HEAD, C_HEAD), F32)

    xa = xa_ref[...]
    hq = _dot(xa, wq_ref[...])
    hf = _dot(xa, wf_ref[...])
    hv = _dot(xa, wi_ref[...])
    hg = _dot(xa, wg_ref[...])

    lbp = lbp_ref[...]
    e = jnp.exp(lbp - jnp.max(lbp, axis=0, keepdims=True))
    sm = e / jnp.sum(e, axis=0, keepdims=True)
    lb = jnp.zeros((1, hw), F32)
    for r in range(1, layer + 1):
        lb = lb + sm[r:r + 1, :]

    f = lb + (1.0 - lb) * jax.nn.sigmoid(hf)
    qq = jax.nn.silu(hq)
    kk = 1.0 - f
    logf = jnp.log(f)
    gate = jax.nn.sigmoid(hg)
    ng = ng_ref[...]

    row = lax.broadcasted_iota(jnp.int32, (C_CHUNK, C_CHUNK), 0)
    col = lax.broadcasted_iota(jnp.int32, (C_CHUNK, C_CHUNK), 1)
    causal = col <= row
    tri = causal.astype(BF16)

    for c in range(tm // C_CHUNK):
        rs = slice(c * C_CHUNK, (c + 1) * C_CHUNK)
        hi, mid, lo = _split3(logf[rs])
        bcum = _dot(tri, hi) + _dot(tri, mid) + _dot(tri, lo)
        blast = bcum[C_CHUNK - 1:C_CHUNK, :]
        q_dec = (qq[rs] * jnp.exp(bcum)).astype(BF16)
        k_dec = (kk[rs] * jnp.exp(-bcum)).astype(BF16)
        k_end = (kk[rs] * jnp.exp(blast - bcum)).astype(BF16)
        dec = jnp.exp(blast)
        vb = hv[rs].astype(BF16)
        for hd in range(heads):
            cs = slice(hd * C_HEAD, (hd + 1) * C_HEAD)
            sid = j * heads + hd
            st = st_ref[sid]
            att = jnp.where(causal, _dot_nt(q_dec[:, cs], k_dec[:, cs]), 0.0)
            o = _dot(att.astype(BF16), vb[:, cs]) + _dot_nt(q_dec[:, cs], st.astype(BF16))
            st_ref[sid] = dec[:, cs] * st + _dot_tn(vb[:, cs], k_end[:, cs])
            o = o * lax.rsqrt(jnp.mean(jnp.square(o), axis=-1, keepdims=True) + LN_EPS) * ng[:, cs]
            y_ref[rs, cs] = (o * gate[rs, cs]).astype(BF16)


def _odd_mixer(x, w_in, lb_param, norm_g, *, layer, seq, tm, hw):
    n, d = x.shape
    d_c = norm_g.shape[0]
    depth = lb_param.shape[0]
    nj = d_c // hw
    assert hw % C_HEAD == 0 and d_c % hw == 0 and tm % C_CHUNK == 0 and seq % tm == 0
    kern = functools.partial(_odd_mixer_kernel, tiles_per_seq=seq // tm, layer=layer)
    wspec = lambda part: pl.BlockSpec((d, hw), lambda i, j: (0, part * nj + j))
    return pl.pallas_call(
        kern,
        grid=(n // tm, nj),
        in_specs=[
            pl.BlockSpec((tm, d), lambda i, j: (i, 0)),
            wspec(0), wspec(1), wspec(2), wspec(3),
            pl.BlockSpec((depth, hw), lambda i, j: (0, j)),
            pl.BlockSpec((1, hw), lambda i, j: (0, j)),
        ],
        out_specs=pl.BlockSpec((tm, hw), lambda i, j: (i, j)),
        out_shape=jax.ShapeDtypeStruct((n, d_c), BF16),
        scratch_shapes=[
            pltpu.VMEM((tm, d), BF16),
            pltpu.VMEM((d_c // C_HEAD, C_HEAD, C_HEAD), F32),
        ],
        compiler_params=pltpu.CompilerParams(
            dimension_semantics=("arbitrary", "arbitrary"), vmem_limit_bytes=VMEM_LIMIT_BYTES),
        name="odd_mixer",
    )(x, w_in, w_in, w_in, w_in, lb_param, norm_g.reshape(1, d_c))


def _odd_pipe_kernel(xb_ref, wq_ref, wf_ref, wi_ref, wg_ref, lbp_ref, ng_ref, y_ref,
                     p0_ref, p1_ref, st_ref, *, nj, tiles_per_seq, layer):
    s = pl.program_id(0)
    tm = xb_ref.shape[0]
    hw = wq_ref.shape[1]
    heads = hw // C_HEAD
    kt = MXU_DIM
    n_k = xb_ref.shape[1] // kt
    n_chunks = tm // C_CHUNK
    prev = jnp.maximum(s - 1, 0)
    ip = prev // nj
    jp = prev % nj

    @pl.when(s == 0)
    def _():
        p1_ref[...] = jnp.zeros(p1_ref.shape, F32)

    @pl.when(ip % tiles_per_seq == 0)
    def _():
        for hd in range(heads):
            st_ref[jp * heads + hd] = jnp.zeros((C_HEAD, C_HEAD), F32)

    def stage(pw_ref, pr_ref):
        lbp = lbp_ref[...]
        e = jnp.exp(lbp - jnp.max(lbp, axis=0, keepdims=True))
        sm = e / jnp.sum(e, axis=0, keepdims=True)
        lb = jnp.zeros((1, hw), F32)
        for r in range(1, layer + 1):
            lb = lb + sm[r:r + 1, :]
        ng = ng_ref[...]
        row = lax.broadcasted_iota(jnp.int32, (C_CHUNK, C_CHUNK), 0)
        col = lax.broadcasted_iota(jnp.int32, (C_CHUNK, C_CHUNK), 1)
        causal = col <= row
        tri = causal.astype(BF16)

        w_refs = (wq_ref, wf_ref, wi_ref, wg_ref)
        pieces = [(part, k) for part in range(len(w_refs)) for k in range(n_k)]
        n_pieces = len(pieces)
        n_slots = n_chunks * (heads + 1)
        partial = {}
        slot = [0]

        def project():
            t = slot[0]
            slot[0] += 1
            for _ in range((t + 1) * n_pieces // n_slots - t * n_pieces // n_slots):
                part, k = pieces.pop(0)
                ks = slice(k * kt, (k + 1) * kt)
                d = _dot(xb_ref[:, ks], w_refs[part][ks, :])
                partial[part] = d if k == 0 else partial[part] + d
                if k == n_k - 1:
                    pw_ref[:, part * hw:(part + 1) * hw] = partial.pop(part)

        head_cols = [slice(hd * C_HEAD, (hd + 1) * C_HEAD) for hd in range(heads)]

        def decays(c):
            rs = slice(c * C_CHUNK, (c + 1) * C_CHUNK)
            f = lb + (1.0 - lb) * jax.nn.sigmoid(pr_ref[rs, hw:2 * hw])
            kk = 1.0 - f
            hi, mid, lo = _split3(jnp.log(f))
            bcum = _dot(tri, hi) + _dot(tri, mid) + _dot(tri, lo)
            blast = bcum[C_CHUNK - 1:C_CHUNK, :]
            return dict(
                q_dec=(jax.nn.silu(pr_ref[rs, 0:hw]) * jnp.exp(bcum)).astype(BF16),
                k_dec=(kk * jnp.exp(-bcum)).astype(BF16),
                k_end=(kk * jnp.exp(blast - bcum)).astype(BF16),
                dec=jnp.exp(blast),
                vb=pr_ref[rs, 2 * hw:3 * hw].astype(BF16),
                gate=jax.nn.sigmoid(pr_ref[rs, 3 * hw:4 * hw]))

        def scores(p):
            att = [jnp.where(causal, _dot_nt(p["q_dec"][:, cs], p["k_dec"][:, cs]), 0.0).astype(BF16)
                   for cs in head_cols]
            upd = [_dot_tn(p["vb"][:, cs], p["k_end"][:, cs]) for cs in head_cols]
            return att, upd

        def outputs(c, p, att, upd):
            rs = slice(c * C_CHUNK, (c + 1) * C_CHUNK)
            for hd, cs in enumerate(head_cols):
                st = states[hd]
                o = _dot(att[hd], p["vb"][:, cs]) + _dot_nt(p["q_dec"][:, cs], st.astype(BF16))
                states[hd] = p["dec"][:, cs] * st + upd[hd]
                o = o * lax.rsqrt(jnp.mean(jnp.square(o), axis=-1, keepdims=True) + LN_EPS) * ng[:, cs]
                y_ref[rs, cs] = (o * p["gate"][:, cs]).astype(BF16)

        states = [st_ref[jp * heads + hd] for hd in range(heads)]
        dec_out, sc_out = {}, {}
        for t in range(n_chunks + 2):
            if t < n_chunks:
                project()
                dec_out[t] = decays(t)
            if 0 <= t - 1 < n_chunks:
                project()
                sc_out[t - 1] = scores(dec_out[t - 1])
            if 0 <= t - 2 < n_chunks:
                project()
                outputs(t - 2, dec_out.pop(t - 2), *sc_out.pop(t - 2))
        assert not pieces and not partial
        for hd in range(heads):
            st_ref[jp * heads + hd] = states[hd]

    @pl.when(s % 2 == 0)
    def _():
        stage(p0_ref, p1_ref)

    @pl.when(s % 2 == 1)
    def _():
        stage(p1_ref, p0_ref)


def _odd_pipe(xb, w_in, lb_param, norm_g, *, layer, seq, tm, hw):
    n, d = xb.shape
    d_c = norm_g.shape[0]
    depth = lb_param.shape[0]
    nj = d_c // hw
    nm = n // tm
    assert hw % C_HEAD == 0 and d_c % hw == 0 and tm % C_CHUNK == 0 and seq % tm == 0
    assert d % MXU_DIM == 0 and n % tm == 0
    kern = functools.partial(_odd_pipe_kernel, nj=nj, tiles_per_seq=seq // tm, layer=layer)
    last = nm * nj - 1
    cur = lambda s: jnp.minimum(s, last)
    prv = lambda s: jnp.maximum(s - 1, 0)
    wspec = lambda part: pl.BlockSpec((d, hw), lambda s: (0, part * nj + cur(s) % nj))
    return pl.pallas_call(
        kern,
        grid=(nm * nj + 1,),
        in_specs=[
            pl.BlockSpec((tm, d), lambda s: (cur(s) // nj, 0)),
            wspec(0), wspec(1), wspec(2), wspec(3),
            pl.BlockSpec((depth, hw), lambda s: (0, prv(s) % nj)),
            pl.BlockSpec((1, hw), lambda s: (0, prv(s) % nj)),
        ],
        out_specs=pl.BlockSpec((tm, hw), lambda s: (prv(s) // nj, prv(s) % nj)),
        out_shape=jax.ShapeDtypeStruct((n, d_c), BF16),
        scratch_shapes=[
            pltpu.VMEM((tm, 4 * hw), F32),
            pltpu.VMEM((tm, 4 * hw), F32),
            pltpu.VMEM((d_c // C_HEAD, C_HEAD, C_HEAD), F32),
        ],
        compiler_params=pltpu.CompilerParams(
            dimension_semantics=("arbitrary",), vmem_limit_bytes=VMEM_LIMIT_BYTES),
        name="odd_mixer",
    )(xb, w_in, w_in, w_in, w_in, lb_param, norm_g.reshape(1, d_c))


def kernel(x, ev_w_in, ev_ln_v_g, ev_ln_v_b, ev_w_s, ev_b_s, ev_w_pool, ev_pool_scale,
           ev_w_out, od_w_in, od_norm_g, od_w_out, lb_param, ffn_w_up, ffn_conv_w,
           ffn_conv_b, ffn_w_down, ln1_g, ln1_b, ln2_g, ln2_b):
    bn, seq, d = x.shape
    depth = ln1_g.shape[0]
    alpha = (2 * depth) ** 0.25
    xf = x.reshape(bn * seq, d)
    ffn_w_up_b = ffn_w_up.astype(BF16)
    ffn_w_down_b = ffn_w_down.astype(BF16)
    for l in range(depth):
        if l % 2 == 0:
            e = l // 2
            y = _even_mixer(xf, ev_w_in[e].astype(BF16), ev_ln_v_g[e], ev_ln_v_b[e], ev_w_s[e],
                            ev_b_s[e], ev_w_pool[e].astype(BF16), ev_pool_scale[e],
                            seq=seq, tm=min(256, seq))
            w_out = ev_w_out[e]
        else:
            o = l // 2
            y = _odd_pipe(xb, od_w_in[o].astype(BF16), lb_param, od_norm_g[o],
                          layer=l, seq=seq, tm=min(512, seq), hw=256)
            w_out = od_w_out[o]
        xf, xb = _proj_ln(y, w_out.astype(BF16), xf, ln1_g[l], ln1_b[l], alpha=alpha, tm=min(512, seq))
        xf, xb = _conv_ffn(xf, xb, ffn_w_up_b, ffn_conv_w, ffn_conv_b, ffn_w_down_b, ln2_g[l], ln2_b[l],
                       layer=l, alpha=alpha, seq=seq, tm=min(512, seq), tf=512)
    return xf.reshape(bn, seq, d)
```

```python
import functools

import jax
import jax.numpy as jnp
from jax import lax
from jax.experimental import pallas as pl
from jax.experimental.pallas import tpu as pltpu

F32 = jnp.float32
BF16 = jnp.bfloat16

A_HEAD = 128
B_WINDOWS = (2, 4, 8, 16)
C_HEAD = 128
C_CHUNK = 64
LN_EPS = 1e-5

SUBLANES = 8
MXU_DIM = 256
UP_PIECE_K = MXU_DIM
GATE_ROWS = 16
VMEM_LIMIT_BYTES = 56 * 1024 * 1024


def _dot(a, b):
    return jnp.dot(a, b, preferred_element_type=F32)


def _dot_nt(a, b):
    return lax.dot_general(a, b, (((1,), (1,)), ((), ())), preferred_element_type=F32)


def _dot_tn(a, b):
    return lax.dot_general(a, b, (((0,), (0,)), ((), ())), preferred_element_type=F32)


def _layer_norm(z, g, b):
    mu = jnp.mean(z, axis=-1, keepdims=True)
    zc = z - mu
    var = jnp.mean(jnp.square(zc), axis=-1, keepdims=True)
    return zc * lax.rsqrt(var + LN_EPS) * g + b


def _proj_ln_kernel(y_ref, w_ref, x_ref, g_ref, b_ref, o_ref, ob_ref, *, alpha):
    mix = _dot(y_ref[...], w_ref[...])
    out = _layer_norm(alpha * x_ref[...] + mix, g_ref[...], b_ref[...])
    o_ref[...] = out
    ob_ref[...] = out.astype(BF16)


def _proj_ln(y, w, x, g, b, *, alpha, tm):
    n, d = x.shape
    k = y.shape[1]
    return pl.pallas_call(
        functools.partial(_proj_ln_kernel, alpha=alpha),
        grid=(n // tm,),
        in_specs=[
            pl.BlockSpec((tm, k), lambda i: (i, 0)),
            pl.BlockSpec((k, d), lambda i: (0, 0)),
            pl.BlockSpec((tm, d), lambda i: (i, 0)),
            pl.BlockSpec((1, d), lambda i: (0, 0)),
            pl.BlockSpec((1, d), lambda i: (0, 0)),
        ],
        out_specs=[pl.BlockSpec((tm, d), lambda i: (i, 0)), pl.BlockSpec((tm, d), lambda i: (i, 0))],
        out_shape=[jax.ShapeDtypeStruct((n, d), F32), jax.ShapeDtypeStruct((n, d), BF16)],
        compiler_params=pltpu.CompilerParams(
            dimension_semantics=("arbitrary",), vmem_limit_bytes=VMEM_LIMIT_BYTES),
        name="proj_ln",
    )(y, w, x, g.reshape(1, d), b.reshape(1, d))


def _conv_ffn_kernel(xb_ref, xr_ref, wa_ref, wv_ref, cwa_ref, cwv_ref, cba_ref, cbv_ref, wd_ref,
                     g_ref, b_ref, o_ref, ob_ref, acc_ref, h0_ref, h1_ref, g0_ref, g1_ref, carry_ref,
                     *, alpha, nf, tiles_per_seq, taps):
    s = pl.program_id(0)
    tm = xb_ref.shape[0]
    tf = wa_ref.shape[1]
    halo = SUBLANES
    i = s // nf
    j = s % nf

    @pl.when(s == 0)
    def _():
        h1_ref[...] = jnp.zeros(h1_ref.shape, F32)
        g0_ref[...] = jnp.zeros(g0_ref.shape, BF16)
        acc_ref[...] = jnp.zeros(acc_ref.shape, F32)
        carry_ref[...] = jnp.zeros(carry_ref.shape, F32)

    def stage(hw_ref, hr_ref, gw_ref, gr_ref):
        seq_start = (i % tiles_per_seq) == 0
        d_in = xb_ref.shape[1]
        kt = UP_PIECE_K
        n_k = d_in // kt
        rb = GATE_ROWS
        per_piece = tm // rb // (2 * n_k)
        d_out = acc_ref.shape[1]

        def gate(blk):
            r0 = blk * rb

            def conv(cs, cw_ref, cb_ref):
                cw = cw_ref[...]
                hc = cb_ref[...] + cw[taps - 1:taps, :] * hr_ref[halo + r0:halo + r0 + rb, cs]
                for tap in range(taps - 1):
                    sh = taps - 1 - tap
                    hc = hc + cw[tap:tap + 1, :] * hr_ref[halo + r0 - sh:halo + r0 - sh + rb, cs]
                return hc

            a = conv(slice(0, tf), cwa_ref, cba_ref)
            v = conv(slice(tf, 2 * tf), cwv_ref, cbv_ref)
            gw_ref[r0:r0 + rb, :] = (jax.nn.silu(a) * v).astype(BF16)

        down_blocks = list(range(0, d_out, MXU_DIM))
        n_down = len(down_blocks)
        n_up = 2 * n_k

        def down(n_blocks):
            for _ in range(n_blocks):
                n0 = down_blocks.pop(0)
                ns = slice(n0, n0 + MXU_DIM)
                acc_ref[:, ns] += _dot(gr_ref[...], wd_ref[:, ns])

        blk = 0
        piece = 0
        for half, w_ref in enumerate((wa_ref, wv_ref)):
            cs = slice(half * tf, (half + 1) * tf)
            h = None
            for k in range(n_k):
                ks = slice(k * kt, (k + 1) * kt)
                part = _dot(xb_ref[:, ks], w_ref[ks, :])
                h = part if h is None else h + part
                for _ in range(per_piece):
                    gate(blk)
                    blk += 1
                piece += 1
                down(piece * n_down // n_up - (piece - 1) * n_down // n_up)
            hw_ref[0:halo, cs] = jnp.where(seq_start, 0.0, carry_ref[j, :, cs])
            hw_ref[halo:halo + tm, cs] = h
            carry_ref[j, :, cs] = h[tm - halo:, :]
        assert not down_blocks

    @pl.when(s % 2 == 0)
    def _():
        stage(h0_ref, h1_ref, g1_ref, g0_ref)

    @pl.when(s % 2 == 1)
    def _():
        stage(h1_ref, h0_ref, g0_ref, g1_ref)

    @pl.when(s < 2)
    def _():
        acc_ref[...] = jnp.zeros(acc_ref.shape, F32)

    @pl.when((s >= 2) & ((s - 2) % nf == nf - 1))
    def _():
        out = _layer_norm(alpha * xr_ref[...] + acc_ref[...], g_ref[...], b_ref[...])
        o_ref[...] = out
        ob_ref[...] = out.astype(BF16)
        acc_ref[...] = jnp.zeros(acc_ref.shape, F32)


def _conv_ffn(x, xb, w_up, conv_w, conv_b, w_down, g, b, *, layer, alpha, seq, tm, tf):
    n, d = x.shape
    d_ff = w_down.shape[1]
    nf = d_ff // tf
    nm = n // tm
    taps = conv_w.shape[1]
    halo = SUBLANES
    assert taps - 1 <= halo and seq % tm == 0 and d_ff % tf == 0 and n % tm == 0
    kern = functools.partial(_conv_ffn_kernel, alpha=alpha, nf=nf, tiles_per_seq=seq // tm, taps=taps)
    last = nm * nf - 1
    up_tile = lambda s: jnp.minimum(s, last) // nf
    up_col = lambda s: jnp.minimum(s, last) % nf
    conv_col = lambda s: jnp.clip(s - 1, 0, last) % nf
    down_tile = lambda s: jnp.maximum(s - 2, 0) // nf
    down_col = lambda s: jnp.maximum(s - 2, 0) % nf
    cb3 = conv_b.reshape(conv_b.shape[0], 1, -1)
    return pl.pallas_call(
        kern,
        grid=(nm * nf + 2,),
        in_specs=[
            pl.BlockSpec((tm, d), lambda s: (up_tile(s), 0)),
            pl.BlockSpec((tm, d), lambda s: (down_tile(s), 0)),
            pl.BlockSpec((None, d, tf), lambda s: (layer, 0, up_col(s))),
            pl.BlockSpec((None, d, tf), lambda s: (layer, 0, nf + up_col(s))),
            pl.BlockSpec((None, taps, tf), lambda s: (layer, 0, conv_col(s))),
            pl.BlockSpec((None, taps, tf), lambda s: (layer, 0, nf + conv_col(s))),
            pl.BlockSpec((None, 1, tf), lambda s: (layer, 0, conv_col(s))),
            pl.BlockSpec((None, 1, tf), lambda s: (layer, 0, nf + conv_col(s))),
            pl.BlockSpec((None, tf, d), lambda s: (layer, down_col(s), 0)),
            pl.BlockSpec((1, d), lambda s: (0, 0)),
            pl.BlockSpec((1, d), lambda s: (0, 0)),
        ],
        out_specs=[pl.BlockSpec((tm, d), lambda s: (down_tile(s), 0)),
                   pl.BlockSpec((tm, d), lambda s: (down_tile(s), 0))],
        out_shape=[jax.ShapeDtypeStruct((n, d), F32), jax.ShapeDtypeStruct((n, d), BF16)],
        scratch_shapes=[
            pltpu.VMEM((tm, d), F32),
            pltpu.VMEM((halo + tm, 2 * tf), F32),
            pltpu.VMEM((halo + tm, 2 * tf), F32),
            pltpu.VMEM((tm, tf), BF16),
            pltpu.VMEM((tm, tf), BF16),
            pltpu.VMEM((nf, halo, 2 * tf), F32),
        ],
        compiler_params=pltpu.CompilerParams(
            dimension_semantics=("arbitrary",), vmem_limit_bytes=VMEM_LIMIT_BYTES),
        name="conv_ffn",
    )(xb, x, w_up, w_up, conv_w, conv_w, cb3, cb3, w_down, g.reshape(1, d), b.reshape(1, d))


def _even_mixer_kernel(x_ref, win_ref, lng_ref, lnb_ref, ws_ref, bst_ref, wp_ref, ps_ref,
                       y_ref, ext_ref, *, tiles_per_seq, d_a):
    i = pl.program_id(0)
    tm = x_ref.shape[0]
    n_heads = ws_ref.shape[0]
    chunk = ws_ref.shape[1]
    n_groups = wp_ref.shape[0]
    grp = wp_ref.shape[1]
    halo = ext_ref.shape[0] - tm

    h = _dot(x_ref[...].astype(BF16), win_ref[...])
    za = jax.nn.gelu(h[:, :2 * d_a])
    u = za[:, :d_a]
    v = _layer_norm(za[:, d_a:], lng_ref[...], lnb_ref[...]).astype(BF16)

    row = lax.broadcasted_iota(jnp.int32, (chunk, chunk), 0)
    col = lax.broadcasted_iota(jnp.int32, (chunk, chunk), 1)
    causal = col <= row
    bst = bst_ref[...]
    for hd in range(n_heads):
        w = jnp.where(causal, ws_ref[hd], 0.0).astype(BF16)
        cs = slice(hd * A_HEAD, (hd + 1) * A_HEAD)
        for c in range(tm // chunk):
            rs = slice(c * chunk, (c + 1) * chunk)
            s = _dot(w, v[rs, cs]) + bst[:, hd:hd + 1]
            y_ref[rs, cs] = (u[rs, cs] * s).astype(BF16)

    xb = h[:, 2 * d_a:]

    @pl.when(i % tiles_per_seq == 0)
    def _():
        ext_ref[0:halo, :] = jnp.zeros((halo, xb.shape[1]), F32)

    @pl.when(i % tiles_per_seq != 0)
    def _():
        ext_ref[0:halo, :] = ext_ref[tm:tm + halo, :]

    ext_ref[halo:halo + tm, :] = xb
    pos = (i % tiles_per_seq) * tm + lax.broadcasted_iota(jnp.int32, (tm, 1), 0)
    posf = (pos + 1).astype(F32)
    for gi in range(n_groups):
        win = B_WINDOWS[gi]
        cs = slice(gi * grp, (gi + 1) * grp)
        wsum = xb[:, cs]
        for k in range(1, win):
            wsum = wsum + ext_ref[halo - k:halo - k + tm, cs]
        cnt = jnp.minimum(posf, float(win))
        p = wsum / cnt - xb[:, cs]
        yb = _dot(p.astype(BF16), wp_ref[gi]) * ps_ref[:, cs]
        y_ref[:, d_a + gi * grp:d_a + (gi + 1) * grp] = yb.astype(BF16)


def _even_mixer(x, w_in, ln_g, ln_b, w_s, b_s, w_pool, pool_scale, *, seq, tm):
    n, d = x.shape
    d_a = ln_g.shape[0]
    d_b = pool_scale.shape[0]
    n_heads, chunk, _ = w_s.shape
    halo = max(B_WINDOWS)
    assert len(B_WINDOWS) == w_pool.shape[0] and n_heads * A_HEAD == d_a
    assert tm % chunk == 0 and seq % tm == 0 and halo % SUBLANES == 0 and halo <= tm
    kern = functools.partial(_even_mixer_kernel, tiles_per_seq=seq // tm, d_a=d_a)
    full = lambda *shape: pl.BlockSpec(shape, lambda i: (0,) * len(shape))
    return pl.pallas_call(
        kern,
        grid=(n // tm,),
        in_specs=[
            pl.BlockSpec((tm, d), lambda i: (i, 0)),
            full(d, 2 * d_a + d_b),
            full(1, d_a),
            full(1, d_a),
            full(*w_s.shape),
            full(chunk, n_heads),
            full(*w_pool.shape),
            full(1, d_b),
        ],
        out_specs=pl.BlockSpec((tm, d_a + d_b), lambda i: (i, 0)),
        out_shape=jax.ShapeDtypeStruct((n, d_a + d_b), BF16),
        scratch_shapes=[pltpu.VMEM((halo + tm, d_b), F32)],
        compiler_params=pltpu.CompilerParams(
            dimension_semantics=("arbitrary",), vmem_limit_bytes=VMEM_LIMIT_BYTES),
        name="even_mixer",
    )(x, w_in, ln_g.reshape(1, d_a), ln_b.reshape(1, d_a), w_s, b_s.T, w_pool,
      pool_scale.reshape(1, d_b))


def _split3(a):
    hi = a.astype(BF16)
    r = a - hi.astype(F32)
    mid = r.astype(BF16)
    lo = (r - mid.astype(F32)).astype(BF16)
    return hi, mid, lo


def _odd_pipe_kernel(xb_ref, wq_ref, wf_ref, wi_ref, wg_ref, lbp_ref, ng_ref, y_ref,
                     p0_ref, p1_ref, st_ref, *, nj, tiles_per_seq, layer):
    s = pl.program_id(0)
    tm = xb_ref.shape[0]
    hw = wq_ref.shape[1]
    heads = hw // C_HEAD
    kt = MXU_DIM
    n_k = xb_ref.shape[1] // kt
    n_chunks = tm // C_CHUNK
    prev = jnp.maximum(s - 1, 0)
    ip = prev // nj
    jp = prev % nj

    @pl.when(s == 0)
    def _():
        p1_ref[...] = jnp.zeros(p1_ref.shape, F32)

    @pl.when(ip % tiles_per_seq == 0)
    def _():
        for hd in range(heads):
            st_ref[jp * heads + hd] = jnp.zeros((C_HEAD, C_HEAD), F32)

    def stage(pw_ref, pr_ref):
        lbp = lbp_ref[...]
        e = jnp.exp(lbp - jnp.max(lbp, axis=0, keepdims=True))
        sm = e / jnp.sum(e, axis=0, keepdims=True)
        lb = jnp.zeros((1, hw), F32)
        for r in range(1, layer + 1):
            lb = lb + sm[r:r + 1, :]
        ng = ng_ref[...]
        row = lax.broadcasted_iota(jnp.int32, (C_CHUNK, C_CHUNK), 0)
        col = lax.broadcasted_iota(jnp.int32, (C_CHUNK, C_CHUNK), 1)
        causal = col <= row
        tri = causal.astype(BF16)

        w_refs = (wq_ref, wf_ref, wi_ref, wg_ref)
        pieces = [(part, k) for part in range(len(w_refs)) for k in range(n_k)]
        n_pieces = len(pieces)
        n_slots = 3 * n_chunks
        partial = {}
        slot = [0]

        def project():
            t = slot[0]
            slot[0] += 1
            for _ in range((t + 1) * n_pieces // n_slots - t * n_pieces // n_slots):
                part, k = pieces.pop(0)
                ks = slice(k * kt, (k + 1) * kt)
                d = _dot(xb_ref[:, ks], w_refs[part][ks, :])
                partial[part] = d if k == 0 else partial[part] + d
                if k == n_k - 1:
                    pw_ref[:, part * hw:(part + 1) * hw] = partial.pop(part)

        head_cols = [slice(hd * C_HEAD, (hd + 1) * C_HEAD) for hd in range(heads)]

        def decays(c):
            rs = slice(c * C_CHUNK, (c + 1) * C_CHUNK)
            f = lb + (1.0 - lb) * jax.nn.sigmoid(pr_ref[rs, hw:2 * hw])
            kk = 1.0 - f
            hi, mid, lo = _split3(jnp.log(f))
            bcum = _dot(tri, hi) + _dot(tri, mid) + _dot(tri, lo)
            blast = bcum[C_CHUNK - 1:C_CHUNK, :]
            return dict(
                q_dec=(jax.nn.silu(pr_ref[rs, 0:hw]) * jnp.exp(bcum)).astype(BF16),
                k_dec=(kk * jnp.exp(-bcum)).astype(BF16),
                k_end=(kk * jnp.exp(blast - bcum)).astype(BF16),
                dec=jnp.exp(blast),
                vb=pr_ref[rs, 2 * hw:3 * hw].astype(BF16),
                gate=jax.nn.sigmoid(pr_ref[rs, 3 * hw:4 * hw]))

        def scores(p):
            att = [jnp.where(causal, _dot_nt(p["q_dec"][:, cs], p["k_dec"][:, cs]), 0.0).astype(BF16)
                   for cs in head_cols]
            upd = [_dot_tn(p["vb"][:, cs], p["k_end"][:, cs]) for cs in head_cols]
            return att, upd

        def outputs(c, p, att, upd):
            rs = slice(c * C_CHUNK, (c + 1) * C_CHUNK)
            for hd, cs in enumerate(head_cols):
                st = states[hd]
                o = _dot(att[hd], p["vb"][:, cs]) + _dot_nt(p["q_dec"][:, cs], st.astype(BF16))
                states[hd] = p["dec"][:, cs] * st + upd[hd]
                o = o * lax.rsqrt(jnp.mean(jnp.square(o), axis=-1, keepdims=True) + LN_EPS) * ng[:, cs]
                y_ref[rs, cs] = (o * p["gate"][:, cs]).astype(BF16)

        states = [st_ref[jp * heads + hd] for hd in range(heads)]
        dec_out, sc_out = {}, {}
        for t in range(n_chunks + 2):
            if t < n_chunks:
                project()
                dec_out[t] = decays(t)
            if 0 <= t - 1 < n_chunks:
                project()
                sc_out[t - 1] = scores(dec_out[t - 1])
            if 0 <= t - 2 < n_chunks:
                project()
                outputs(t - 2, dec_out.pop(t - 2), *sc_out.pop(t - 2))
        assert not pieces and not partial
        for hd in range(heads):
            st_ref[jp * heads + hd] = states[hd]

    @pl.when(s % 2 == 0)
    def _():
        stage(p0_ref, p1_ref)

    @pl.when(s % 2 == 1)
    def _():
        stage(p1_ref, p0_ref)


def _odd_pipe(xb, w_in, lb_param, norm_g, *, layer, seq, tm, hw):
    n, d = xb.shape
    d_c = norm_g.shape[0]
    depth = lb_param.shape[0]
    nj = d_c // hw
    nm = n // tm
    assert hw % C_HEAD == 0 and d_c % hw == 0 and tm % C_CHUNK == 0 and seq % tm == 0
    assert d % MXU_DIM == 0 and n % tm == 0
    kern = functools.partial(_odd_pipe_kernel, nj=nj, tiles_per_seq=seq // tm, layer=layer)
    last = nm * nj - 1
    cur = lambda s: jnp.minimum(s, last)
    prv = lambda s: jnp.maximum(s - 1, 0)
    wspec = lambda part: pl.BlockSpec((d, hw), lambda s: (0, part * nj + cur(s) % nj))
    return pl.pallas_call(
        kern,
        grid=(nm * nj + 1,),
        in_specs=[
            pl.BlockSpec((tm, d), lambda s: (cur(s) // nj, 0)),
            wspec(0), wspec(1), wspec(2), wspec(3),
            pl.BlockSpec((depth, hw), lambda s: (0, prv(s) % nj)),
            pl.BlockSpec((1, hw), lambda s: (0, prv(s) % nj)),
        ],
        out_specs=pl.BlockSpec((tm, hw), lambda s: (prv(s) // nj, prv(s) % nj)),
        out_shape=jax.ShapeDtypeStruct((n, d_c), BF16),
        scratch_shapes=[
            pltpu.VMEM((tm, 4 * hw), F32),
            pltpu.VMEM((tm, 4 * hw), F32),
            pltpu.VMEM((d_c // C_HEAD, C_HEAD, C_HEAD), F32),
        ],
        compiler_params=pltpu.CompilerParams(
            dimension_semantics=("arbitrary",), vmem_limit_bytes=VMEM_LIMIT_BYTES),
        name="odd_mixer",
    )(xb, w_in, w_in, w_in, w_in, lb_param, norm_g.reshape(1, d_c))


def kernel(x, ev_w_in, ev_ln_v_g, ev_ln_v_b, ev_w_s, ev_b_s, ev_w_pool, ev_pool_scale,
           ev_w_out, od_w_in, od_norm_g, od_w_out, lb_param, ffn_w_up, ffn_conv_w,
           ffn_conv_b, ffn_w_down, ln1_g, ln1_b, ln2_g, ln2_b):
    bn, seq, d = x.shape
    depth = ln1_g.shape[0]
    alpha = (2 * depth) ** 0.25
    xf = x.reshape(bn * seq, d)
    ffn_w_up_b = ffn_w_up.astype(BF16)
    ffn_w_down_b = ffn_w_down.astype(BF16)
    for l in range(depth):
        if l % 2 == 0:
            e = l // 2
            y = _even_mixer(xf, ev_w_in[e].astype(BF16), ev_ln_v_g[e], ev_ln_v_b[e], ev_w_s[e],
                            ev_b_s[e], ev_w_pool[e].astype(BF16), ev_pool_scale[e],
                            seq=seq, tm=min(256, seq))
            w_out = ev_w_out[e]
        else:
            o = l // 2
            y = _odd_pipe(xb, od_w_in[o].astype(BF16), lb_param, od_norm_g[o],
                          layer=l, seq=seq, tm=min(512, seq), hw=512)
            w_out = od_w_out[o]
        xf, xb = _proj_ln(y, w_out.astype(BF16), xf, ln1_g[l], ln1_b[l], alpha=alpha, tm=min(512, seq))
        xf, xb = _conv_ffn(xf, xb, ffn_w_up_b, ffn_conv_w, ffn_conv_b, ffn_w_down_b, ln2_g[l], ln2_b[l],
                       layer=l, alpha=alpha, seq=seq, tm=min(512, seq), tf=512)
    return xf.reshape(bn, seq, d)
```

```python
import functools

import jax
import jax.numpy as jnp
from jax import lax
from jax.experimental import pallas as pl
from jax.experimental.pallas import tpu as pltpu

F32 = jnp.float32
BF16 = jnp.bfloat16

A_HEAD = 128
B_WINDOWS = (2, 4, 8, 16)
C_HEAD = 128
C_CHUNK = 64
LN_EPS = 1e-5

SUBLANES = 8
MXU_DIM = 256
UP_PIECE_K = MXU_DIM
FFN_COL_TILE = 512
ODD_GROUP_WIDTH = 2 * C_HEAD
GATE_ROWS = 16
VMEM_LIMIT_BYTES = 56 * 1024 * 1024


def _dot(a, b):
    return jnp.dot(a, b, preferred_element_type=F32)


def _dot_nt(a, b):
    return lax.dot_general(a, b, (((1,), (1,)), ((), ())), preferred_element_type=F32)


def _dot_tn(a, b):
    return lax.dot_general(a, b, (((0,), (0,)), ((), ())), preferred_element_type=F32)


def _layer_norm(z, g, b):
    mu = jnp.mean(z, axis=-1, keepdims=True)
    zc = z - mu
    var = jnp.mean(jnp.square(zc), axis=-1, keepdims=True)
    return zc * lax.rsqrt(var + LN_EPS) * g + b


def _proj_ln_kernel(y_ref, w_ref, x_ref, g_ref, b_ref, o_ref, ob_ref, *, alpha):
    mix = _dot(y_ref[...], w_ref[...])
    out = _layer_norm(alpha * x_ref[...] + mix, g_ref[...], b_ref[...])
    o_ref[...] = out
    ob_ref[...] = out.astype(BF16)


def _proj_ln(y, w, x, g, b, *, alpha, tm):
    n, d = x.shape
    k = y.shape[1]
    return pl.pallas_call(
        functools.partial(_proj_ln_kernel, alpha=alpha),
        grid=(n // tm,),
        in_specs=[
            pl.BlockSpec((tm, k), lambda i: (i, 0)),
            pl.BlockSpec((k, d), lambda i: (0, 0)),
            pl.BlockSpec((tm, d), lambda i: (i, 0)),
            pl.BlockSpec((1, d), lambda i: (0, 0)),
            pl.BlockSpec((1, d), lambda i: (0, 0)),
        ],
        out_specs=[pl.BlockSpec((tm, d), lambda i: (i, 0)), pl.BlockSpec((tm, d), lambda i: (i, 0))],
        out_shape=[jax.ShapeDtypeStruct((n, d), F32), jax.ShapeDtypeStruct((n, d), BF16)],
        compiler_params=pltpu.CompilerParams(
            dimension_semantics=("arbitrary",), vmem_limit_bytes=VMEM_LIMIT_BYTES),
        name="proj_ln",
    )(y, w, x, g.reshape(1, d), b.reshape(1, d))


def _conv_ffn_kernel(xb_ref, xr_ref, wu_ref, cp_ref, wd_ref,
                     g_ref, b_ref, o_ref, ob_ref, acc_ref, h0_ref, h1_ref, g0_ref, g1_ref, carry_ref,
                     *, alpha, nf, tiles_per_seq, taps):
    s = pl.program_id(0)
    tm = xb_ref.shape[0]
    tf = wu_ref.shape[1] // 2
    halo = SUBLANES
    i = s // nf
    j = s % nf

    @pl.when(s == 0)
    def _():
        h1_ref[...] = jnp.zeros(h1_ref.shape, F32)
        g0_ref[...] = jnp.zeros(g0_ref.shape, BF16)
        acc_ref[...] = jnp.zeros(acc_ref.shape, F32)
        carry_ref[...] = jnp.zeros(carry_ref.shape, F32)

    def stage(hw_ref, hr_ref, gw_ref, gr_ref):
        seq_start = (i % tiles_per_seq) == 0
        d_in = xb_ref.shape[1]
        kt = UP_PIECE_K
        n_k = d_in // kt
        rb = GATE_ROWS
        per_piece = tm // rb // (2 * n_k)
        d_out = acc_ref.shape[1]

        def gate(blk):
            r0 = blk * rb

            def conv(cs):
                cp = cp_ref[:, cs]
                hc = cp[taps:taps + 1, :] + cp[taps - 1:taps, :] * hr_ref[halo + r0:halo + r0 + rb, cs]
                for tap in range(taps - 1):
                    sh = taps - 1 - tap
                    hc = hc + cp[tap:tap + 1, :] * hr_ref[halo + r0 - sh:halo + r0 - sh + rb, cs]
                return hc

            a = conv(slice(0, tf))
            v = conv(slice(tf, 2 * tf))
            gw_ref[r0:r0 + rb, :] = (jax.nn.silu(a) * v).astype(BF16)

        down_blocks = list(range(0, d_out, MXU_DIM))
        n_down = len(down_blocks)
        n_up = 2 * n_k

        def down(n_blocks):
            for _ in range(n_blocks):
                n0 = down_blocks.pop(0)
                ns = slice(n0, n0 + MXU_DIM)
                acc_ref[:, ns] += _dot(gr_ref[...], wd_ref[:, ns])

        blk = 0
        piece = 0
        for half in range(2):
            cs = slice(half * tf, (half + 1) * tf)
            h = None
            for k in range(n_k):
                ks = slice(k * kt, (k + 1) * kt)
                part = _dot(xb_ref[:, ks], wu_ref[ks, cs])
                h = part if h is None else h + part
                for _ in range(per_piece):
                    gate(blk)
                    blk += 1
                piece += 1
                down(piece * n_down // n_up - (piece - 1) * n_down // n_up)
            hw_ref[0:halo, cs] = jnp.where(seq_start, 0.0, carry_ref[j, :, cs])
            hw_ref[halo:halo + tm, cs] = h
            carry_ref[j, :, cs] = h[tm - halo:, :]
        assert not down_blocks

    @pl.when(s % 2 == 0)
    def _():
        stage(h0_ref, h1_ref, g1_ref, g0_ref)

    @pl.when(s % 2 == 1)
    def _():
        stage(h1_ref, h0_ref, g0_ref, g1_ref)

    @pl.when(s < 2)
    def _():
        acc_ref[...] = jnp.zeros(acc_ref.shape, F32)

    @pl.when((s >= 2) & ((s - 2) % nf == nf - 1))
    def _():
        out = _layer_norm(alpha * xr_ref[...] + acc_ref[...], g_ref[...], b_ref[...])
        o_ref[...] = out
        ob_ref[...] = out.astype(BF16)
        acc_ref[...] = jnp.zeros(acc_ref.shape, F32)


def _tile_columns(a, parts, nt, tw):
    lead, rows = a.shape[:-2], a.shape[-2]
    a = a.reshape(*lead, rows, parts, nt, tw)
    a = jnp.moveaxis(a, -2, -4)
    return a.reshape(*lead, nt, rows, parts * tw)


def _conv_ffn(x, xb, w_up_t, conv_p_t, w_down, g, b, *, layer, alpha, seq, tm):
    n, d = x.shape
    nf, tf = w_up_t.shape[1], w_up_t.shape[3] // 2
    nm = n // tm
    taps = conv_p_t.shape[2] - 1
    halo = SUBLANES
    assert taps - 1 <= halo and seq % tm == 0 and n % tm == 0 and w_down.shape[1] == nf * tf
    kern = functools.partial(_conv_ffn_kernel, alpha=alpha, nf=nf, tiles_per_seq=seq // tm, taps=taps)
    last = nm * nf - 1
    up_tile = lambda s: jnp.minimum(s, last) // nf
    up_col = lambda s: jnp.minimum(s, last) % nf
    conv_col = lambda s: jnp.clip(s - 1, 0, last) % nf
    down_tile = lambda s: jnp.maximum(s - 2, 0) // nf
    down_col = lambda s: jnp.maximum(s - 2, 0) % nf
    return pl.pallas_call(
        kern,
        grid=(nm * nf + 2,),
        in_specs=[
            pl.BlockSpec((tm, d), lambda s: (up_tile(s), 0)),
            pl.BlockSpec((tm, d), lambda s: (down_tile(s), 0)),
            pl.BlockSpec((None, None, d, 2 * tf), lambda s: (layer, up_col(s), 0, 0)),
            pl.BlockSpec((None, None, taps + 1, 2 * tf), lambda s: (layer, conv_col(s), 0, 0)),
            pl.BlockSpec((None, tf, d), lambda s: (layer, down_col(s), 0)),
            pl.BlockSpec((1, d), lambda s: (0, 0)),
            pl.BlockSpec((1, d), lambda s: (0, 0)),
        ],
        out_specs=[pl.BlockSpec((tm, d), lambda s: (down_tile(s), 0)),
                   pl.BlockSpec((tm, d), lambda s: (down_tile(s), 0))],
        out_shape=[jax.ShapeDtypeStruct((n, d), F32), jax.ShapeDtypeStruct((n, d), BF16)],
        scratch_shapes=[
            pltpu.VMEM((tm, d), F32),
            pltpu.VMEM((halo + tm, 2 * tf), F32),
            pltpu.VMEM((halo + tm, 2 * tf), F32),
            pltpu.VMEM((tm, tf), BF16),
            pltpu.VMEM((tm, tf), BF16),
            pltpu.VMEM((nf, halo, 2 * tf), F32),
        ],
        compiler_params=pltpu.CompilerParams(
            dimension_semantics=("arbitrary",), vmem_limit_bytes=VMEM_LIMIT_BYTES),
        name="conv_ffn",
    )(xb, x, w_up_t, conv_p_t, w_down, g.reshape(1, d), b.reshape(1, d))


def _even_mixer_kernel(x_ref, win_ref, lng_ref, lnb_ref, ws_ref, bst_ref, wp_ref, ps_ref,
                       y_ref, ext_ref, *, tiles_per_seq, d_a):
    i = pl.program_id(0)
    tm = x_ref.shape[0]
    n_heads = ws_ref.shape[0]
    chunk = ws_ref.shape[1]
    n_groups = wp_ref.shape[0]
    grp = wp_ref.shape[1]
    halo = ext_ref.shape[0] - tm

    h = _dot(x_ref[...].astype(BF16), win_ref[...])
    za = jax.nn.gelu(h[:, :2 * d_a])
    u = za[:, :d_a]
    v = _layer_norm(za[:, d_a:], lng_ref[...], lnb_ref[...]).astype(BF16)

    row = lax.broadcasted_iota(jnp.int32, (chunk, chunk), 0)
    col = lax.broadcasted_iota(jnp.int32, (chunk, chunk), 1)
    causal = col <= row
    bst = bst_ref[...]
    for hd in range(n_heads):
        w = jnp.where(causal, ws_ref[hd], 0.0).astype(BF16)
        cs = slice(hd * A_HEAD, (hd + 1) * A_HEAD)
        for c in range(tm // chunk):
            rs = slice(c * chunk, (c + 1) * chunk)
            s = _dot(w, v[rs, cs]) + bst[:, hd:hd + 1]
            y_ref[rs, cs] = (u[rs, cs] * s).astype(BF16)

    xb = h[:, 2 * d_a:]

    @pl.when(i % tiles_per_seq == 0)
    def _():
        ext_ref[0:halo, :] = jnp.zeros((halo, xb.shape[1]), F32)

    @pl.when(i % tiles_per_seq != 0)
    def _():
        ext_ref[0:halo, :] = ext_ref[tm:tm + halo, :]

    ext_ref[halo:halo + tm, :] = xb
    pos = (i % tiles_per_seq) * tm + lax.broadcasted_iota(jnp.int32, (tm, 1), 0)
    posf = (pos + 1).astype(F32)
    for gi in range(n_groups):
        win = B_WINDOWS[gi]
        cs = slice(gi * grp, (gi + 1) * grp)
        wsum = xb[:, cs]
        for k in range(1, win):
            wsum = wsum + ext_ref[halo - k:halo - k + tm, cs]
        cnt = jnp.minimum(posf, float(win))
        p = wsum / cnt - xb[:, cs]
        yb = _dot(p.astype(BF16), wp_ref[gi]) * ps_ref[:, cs]
        y_ref[:, d_a + gi * grp:d_a + (gi + 1) * grp] = yb.astype(BF16)


def _even_mixer(x, w_in, ln_g, ln_b, w_s, b_s, w_pool, pool_scale, *, seq, tm):
    n, d = x.shape
    d_a = ln_g.shape[0]
    d_b = pool_scale.shape[0]
    n_heads, chunk, _ = w_s.shape
    halo = max(B_WINDOWS)
    assert len(B_WINDOWS) == w_pool.shape[0] and n_heads * A_HEAD == d_a
    assert tm % chunk == 0 and seq % tm == 0 and halo % SUBLANES == 0 and halo <= tm
    kern = functools.partial(_even_mixer_kernel, tiles_per_seq=seq // tm, d_a=d_a)
    full = lambda *shape: pl.BlockSpec(shape, lambda i: (0,) * len(shape))
    return pl.pallas_call(
        kern,
        grid=(n // tm,),
        in_specs=[
            pl.BlockSpec((tm, d), lambda i: (i, 0)),
            full(d, 2 * d_a + d_b),
            full(1, d_a),
            full(1, d_a),
            full(*w_s.shape),
            full(chunk, n_heads),
            full(*w_pool.shape),
            full(1, d_b),
        ],
        out_specs=pl.BlockSpec((tm, d_a + d_b), lambda i: (i, 0)),
        out_shape=jax.ShapeDtypeStruct((n, d_a + d_b), BF16),
        scratch_shapes=[pltpu.VMEM((halo + tm, d_b), F32)],
        compiler_params=pltpu.CompilerParams(
            dimension_semantics=("arbitrary",), vmem_limit_bytes=VMEM_LIMIT_BYTES),
        name="even_mixer",
    )(x, w_in, ln_g.reshape(1, d_a), ln_b.reshape(1, d_a), w_s, b_s.T, w_pool,
      pool_scale.reshape(1, d_b))


def _split3(a):
    hi = a.astype(BF16)
    r = a - hi.astype(F32)
    mid = r.astype(BF16)
    lo = (r - mid.astype(F32)).astype(BF16)
    return hi, mid, lo


def _odd_pipe_kernel(xb_ref, w_ref, lbp_ref, ng_ref, y_ref,
                     p0_ref, p1_ref, st_ref, *, nj, tiles_per_seq, layer):
    s = pl.program_id(0)
    tm = xb_ref.shape[0]
    n_parts = 4
    hw = w_ref.shape[1] // n_parts
    heads = hw // C_HEAD
    kt = MXU_DIM
    n_k = xb_ref.shape[1] // kt
    n_chunks = tm // C_CHUNK
    prev = jnp.maximum(s - 1, 0)
    ip = prev // nj
    jp = prev % nj

    @pl.when(s == 0)
    def _():
        p1_ref[...] = jnp.zeros(p1_ref.shape, F32)

    @pl.when(ip % tiles_per_seq == 0)
    def _():
        for hd in range(heads):
            st_ref[jp * heads + hd] = jnp.zeros((C_HEAD, C_HEAD), F32)

    def stage(pw_ref, pr_ref):
        lbp = lbp_ref[...]
        e = jnp.exp(lbp - jnp.max(lbp, axis=0, keepdims=True))
        sm = e / jnp.sum(e, axis=0, keepdims=True)
        lb = jnp.zeros((1, hw), F32)
        for r in range(1, layer + 1):
            lb = lb + sm[r:r + 1, :]
        ng = ng_ref[...]
        row = lax.broadcasted_iota(jnp.int32, (C_CHUNK, C_CHUNK), 0)
        col = lax.broadcasted_iota(jnp.int32, (C_CHUNK, C_CHUNK), 1)
        causal = col <= row
        tri = causal.astype(BF16)

        pieces = [(part, k) for part in range(n_parts) for k in range(n_k)]
        n_pieces = len(pieces)
        n_slots = 3 * n_chunks
        partial = {}
        slot = [0]

        def project():
            t = slot[0]
            slot[0] += 1
            for _ in range((t + 1) * n_pieces // n_slots - t * n_pieces // n_slots):
                part, k = pieces.pop(0)
                ks = slice(k * kt, (k + 1) * kt)
                ps = slice(part * hw, (part + 1) * hw)
                d = _dot(xb_ref[:, ks], w_ref[ks, ps])
                partial[part] = d if k == 0 else partial[part] + d
                if k == n_k - 1:
                    pw_ref[:, ps] = partial.pop(part)

        head_cols = [slice(hd * C_HEAD, (hd + 1) * C_HEAD) for hd in range(heads)]

        def decays(c):
            rs = slice(c * C_CHUNK, (c + 1) * C_CHUNK)
            f = lb + (1.0 - lb) * jax.nn.sigmoid(pr_ref[rs, hw:2 * hw])
            kk = 1.0 - f
            hi, mid, lo = _split3(jnp.log(f))
            bcum = _dot(tri, hi) + _dot(tri, mid) + _dot(tri, lo)
            blast = bcum[C_CHUNK - 1:C_CHUNK, :]
            return dict(
                q_dec=(jax.nn.silu(pr_ref[rs, 0:hw]) * jnp.exp(bcum)).astype(BF16),
                k_dec=(kk * jnp.exp(-bcum)).astype(BF16),
                k_end=(kk * jnp.exp(blast - bcum)).astype(BF16),
                dec=jnp.exp(blast),
                vb=pr_ref[rs, 2 * hw:3 * hw].astype(BF16),
                gate=jax.nn.sigmoid(pr_ref[rs, 3 * hw:4 * hw]))

        def scores(p):
            att = [jnp.where(causal, _dot_nt(p["q_dec"][:, cs], p["k_dec"][:, cs]), 0.0).astype(BF16)
                   for cs in head_cols]
            upd = [_dot_tn(p["vb"][:, cs], p["k_end"][:, cs]) for cs in head_cols]
            return att, upd

        def outputs(c, p, att, upd):
            rs = slice(c * C_CHUNK, (c + 1) * C_CHUNK)
            for hd, cs in enumerate(head_cols):
                st = states[hd]
                o = _dot(att[hd], p["vb"][:, cs]) + _dot_nt(p["q_dec"][:, cs], st.astype(BF16))
                states[hd] = p["dec"][:, cs] * st + upd[hd]
                o = o * lax.rsqrt(jnp.mean(jnp.square(o), axis=-1, keepdims=True) + LN_EPS) * ng[:, cs]
                y_ref[rs, cs] = (o * p["gate"][:, cs]).astype(BF16)

        states = [st_ref[jp * heads + hd] for hd in range(heads)]
        dec_out, sc_out = {}, {}
        for t in range(n_chunks + 2):
            if t < n_chunks:
                project()
                dec_out[t] = decays(t)
            if 0 <= t - 1 < n_chunks:
                project()
                sc_out[t - 1] = scores(dec_out[t - 1])
            if 0 <= t - 2 < n_chunks:
                project()
                outputs(t - 2, dec_out.pop(t - 2), *sc_out.pop(t - 2))
        assert not pieces and not partial
        for hd in range(heads):
            st_ref[jp * heads + hd] = states[hd]

    @pl.when(s % 2 == 0)
    def _():
        stage(p0_ref, p1_ref)

    @pl.when(s % 2 == 1)
    def _():
        stage(p1_ref, p0_ref)


def _odd_pipe(xb, w_in_t, lb_param, norm_g, *, layer, seq, tm):
    n, d = xb.shape
    d_c = norm_g.shape[0]
    depth = lb_param.shape[0]
    nj, hw = w_in_t.shape[0], w_in_t.shape[2] // 4
    nm = n // tm
    assert hw % C_HEAD == 0 and nj * hw == d_c and tm % C_CHUNK == 0 and seq % tm == 0
    assert d % MXU_DIM == 0 and n % tm == 0
    kern = functools.partial(_odd_pipe_kernel, nj=nj, tiles_per_seq=seq // tm, layer=layer)
    last = nm * nj - 1
    cur = lambda s: jnp.minimum(s, last)
    prv = lambda s: jnp.maximum(s - 1, 0)
    return pl.pallas_call(
        kern,
        grid=(nm * nj + 1,),
        in_specs=[
            pl.BlockSpec((tm, d), lambda s: (cur(s) // nj, 0)),
            pl.BlockSpec((None, d, 4 * hw), lambda s: (cur(s) % nj, 0, 0)),
            pl.BlockSpec((depth, hw), lambda s: (0, prv(s) % nj)),
            pl.BlockSpec((1, hw), lambda s: (0, prv(s) % nj)),
        ],
        out_specs=pl.BlockSpec((tm, hw), lambda s: (prv(s) // nj, prv(s) % nj)),
        out_shape=jax.ShapeDtypeStruct((n, d_c), BF16),
        scratch_shapes=[
            pltpu.VMEM((tm, 4 * hw), F32),
            pltpu.VMEM((tm, 4 * hw), F32),
            pltpu.VMEM((d_c // C_HEAD, C_HEAD, C_HEAD), F32),
        ],
        compiler_params=pltpu.CompilerParams(
            dimension_semantics=("arbitrary",), vmem_limit_bytes=VMEM_LIMIT_BYTES),
        name="odd_mixer",
    )(xb, w_in_t, lb_param, norm_g.reshape(1, d_c))


def kernel(x, ev_w_in, ev_ln_v_g, ev_ln_v_b, ev_w_s, ev_b_s, ev_w_pool, ev_pool_scale,
           ev_w_out, od_w_in, od_norm_g, od_w_out, lb_param, ffn_w_up, ffn_conv_w,
           ffn_conv_b, ffn_w_down, ln1_g, ln1_b, ln2_g, ln2_b):
    bn, seq, d = x.shape
    depth = ln1_g.shape[0]
    alpha = (2 * depth) ** 0.25
    xf = x.reshape(bn * seq, d)
    tf = FFN_COL_TILE
    nf = ffn_w_down.shape[1] // tf
    ffn_w_up_t = _tile_columns(ffn_w_up.astype(BF16), 2, nf, tf)
    ffn_conv_t = _tile_columns(jnp.concatenate([ffn_conv_w, ffn_conv_b[:, None, :]], axis=1), 2, nf, tf)
    ffn_w_down_b = ffn_w_down.astype(BF16)
    for l in range(depth):
        if l % 2 == 0:
            e = l // 2
            y = _even_mixer(xf, ev_w_in[e].astype(BF16), ev_ln_v_g[e], ev_ln_v_b[e], ev_w_s[e],
                            ev_b_s[e], ev_w_pool[e].astype(BF16), ev_pool_scale[e],
                            seq=seq, tm=min(256, seq))
            w_out = ev_w_out[e]
        else:
            o = l // 2
            hw = ODD_GROUP_WIDTH
            w_in_t = _tile_columns(od_w_in[o].astype(BF16), 4, od_norm_g.shape[1] // hw, hw)
            y = _odd_pipe(xb, w_in_t, lb_param, od_norm_g[o], layer=l, seq=seq, tm=min(512, seq))
            w_out = od_w_out[o]
        xf, xb = _proj_ln(y, w_out.astype(BF16), xf, ln1_g[l], ln1_b[l], alpha=alpha, tm=min(512, seq))
        xf, xb = _conv_ffn(xf, xb, ffn_w_up_t, ffn_conv_t, ffn_w_down_b, ln2_g[l], ln2_b[l],
                           layer=l, alpha=alpha, seq=seq, tm=min(512, seq))
    return xf.reshape(bn, seq, d)
```

```python
import functools

import jax
import jax.numpy as jnp
from jax import lax
from jax.experimental import pallas as pl
from jax.experimental.pallas import tpu as pltpu

F32 = jnp.float32
BF16 = jnp.bfloat16

A_HEAD = 128
B_WINDOWS = (2, 4, 8, 16)
C_HEAD = 128
C_CHUNK = 64
LN_EPS = 1e-5

SUBLANES = 8
MXU_DIM = 256
UP_PIECE_K = MXU_DIM
FFN_COL_TILE = 512
ODD_GROUP_WIDTH = 2 * C_HEAD
GATE_ROWS = 16
VMEM_LIMIT_BYTES = 56 * 1024 * 1024


def _dot(a, b):
    return jnp.dot(a, b, preferred_element_type=F32)


def _dot_nt(a, b):
    return lax.dot_general(a, b, (((1,), (1,)), ((), ())), preferred_element_type=F32)


def _dot_tn(a, b):
    return lax.dot_general(a, b, (((0,), (0,)), ((), ())), preferred_element_type=F32)


def _layer_norm(z, g, b):
    mu = jnp.mean(z, axis=-1, keepdims=True)
    zc = z - mu
    var = jnp.mean(jnp.square(zc), axis=-1, keepdims=True)
    return zc * lax.rsqrt(var + LN_EPS) * g + b


def _proj_ln_kernel(y_ref, w_ref, x_ref, g_ref, b_ref, o_ref, ob_ref, *, alpha):
    mix = _dot(y_ref[...], w_ref[...])
    out = _layer_norm(alpha * x_ref[...] + mix, g_ref[...], b_ref[...])
    o_ref[...] = out
    ob_ref[...] = out.astype(BF16)


def _proj_ln(y, w, x, g, b, *, alpha, tm):
    n, d = x.shape
    k = y.shape[1]
    return pl.pallas_call(
        functools.partial(_proj_ln_kernel, alpha=alpha),
        grid=(n // tm,),
        in_specs=[
            pl.BlockSpec((tm, k), lambda i: (i, 0)),
            pl.BlockSpec((k, d), lambda i: (0, 0)),
            pl.BlockSpec((tm, d), lambda i: (i, 0)),
            pl.BlockSpec((1, d), lambda i: (0, 0)),
            pl.BlockSpec((1, d), lambda i: (0, 0)),
        ],
        out_specs=[pl.BlockSpec((tm, d), lambda i: (i, 0)), pl.BlockSpec((tm, d), lambda i: (i, 0))],
        out_shape=[jax.ShapeDtypeStruct((n, d), F32), jax.ShapeDtypeStruct((n, d), BF16)],
        compiler_params=pltpu.CompilerParams(
            dimension_semantics=("arbitrary",), vmem_limit_bytes=VMEM_LIMIT_BYTES),
        name="proj_ln",
    )(y, w, x, g.reshape(1, d), b.reshape(1, d))


def _conv_ffn_kernel(xb_ref, xr_ref, wu_ref, cpa_ref, cpv_ref, wd_ref,
                     g_ref, b_ref, o_ref, ob_ref, acc_ref, h0_ref, h1_ref, g0_ref, g1_ref, carry_ref,
                     *, alpha, nf, tiles_per_seq, taps):
    s = pl.program_id(0)
    tm = xb_ref.shape[0]
    tf = wu_ref.shape[1] // 2
    halo = SUBLANES
    i = s // nf
    j = s % nf

    @pl.when(s == 0)
    def _():
        h1_ref[...] = jnp.zeros(h1_ref.shape, F32)
        g0_ref[...] = jnp.zeros(g0_ref.shape, BF16)
        acc_ref[...] = jnp.zeros(acc_ref.shape, F32)
        carry_ref[...] = jnp.zeros(carry_ref.shape, F32)

    def stage(hw_ref, hr_ref, gw_ref, gr_ref):
        seq_start = (i % tiles_per_seq) == 0
        d_in = xb_ref.shape[1]
        kt = UP_PIECE_K
        n_k = d_in // kt
        rb = GATE_ROWS
        per_piece = tm // rb // (2 * n_k)
        d_out = acc_ref.shape[1]

        def gate(blk):
            r0 = blk * rb

            def conv(cs, cp_ref):
                cp = cp_ref[...]
                hc = cp[taps:taps + 1, :] + cp[taps - 1:taps, :] * hr_ref[halo + r0:halo + r0 + rb, cs]
                for tap in range(taps - 1):
                    sh = taps - 1 - tap
                    hc = hc + cp[tap:tap + 1, :] * hr_ref[halo + r0 - sh:halo + r0 - sh + rb, cs]
                return hc

            a = conv(slice(0, tf), cpa_ref)
            v = conv(slice(tf, 2 * tf), cpv_ref)
            gw_ref[r0:r0 + rb, :] = (jax.nn.silu(a) * v).astype(BF16)

        down_blocks = list(range(0, d_out, MXU_DIM))
        n_down = len(down_blocks)
        n_up = 2 * n_k

        def down(n_blocks):
            for _ in range(n_blocks):
                n0 = down_blocks.pop(0)
                ns = slice(n0, n0 + MXU_DIM)
                acc_ref[:, ns] += _dot(gr_ref[...], wd_ref[:, ns])

        blk = 0
        piece = 0
        for half in range(2):
            cs = slice(half * tf, (half + 1) * tf)
            h = None
            for k in range(n_k):
                ks = slice(k * kt, (k + 1) * kt)
                part = _dot(xb_ref[:, ks], wu_ref[ks, cs])
                h = part if h is None else h + part
                for _ in range(per_piece):
                    gate(blk)
                    blk += 1
                piece += 1
                down(piece * n_down // n_up - (piece - 1) * n_down // n_up)
            hw_ref[0:halo, cs] = jnp.where(seq_start, 0.0, carry_ref[j, :, cs])
            hw_ref[halo:halo + tm, cs] = h
            carry_ref[j, :, cs] = h[tm - halo:, :]
        assert not down_blocks

    @pl.when(s % 2 == 0)
    def _():
        stage(h0_ref, h1_ref, g1_ref, g0_ref)

    @pl.when(s % 2 == 1)
    def _():
        stage(h1_ref, h0_ref, g0_ref, g1_ref)

    @pl.when(s < 2)
    def _():
        acc_ref[...] = jnp.zeros(acc_ref.shape, F32)

    @pl.when((s >= 2) & ((s - 2) % nf == nf - 1))
    def _():
        out = _layer_norm(alpha * xr_ref[...] + acc_ref[...], g_ref[...], b_ref[...])
        o_ref[...] = out
        ob_ref[...] = out.astype(BF16)
        acc_ref[...] = jnp.zeros(acc_ref.shape, F32)


def _tile_cast_kernel(a_ref, o_ref):
    o_ref[...] = a_ref[...].astype(o_ref.dtype)


def _tile_columns(a, parts, nt, tw, dtype):
    n_l, rows, cols = a.shape
    assert cols == parts * nt * tw
    return pl.pallas_call(
        _tile_cast_kernel,
        grid=(n_l, nt, parts),
        in_specs=[pl.BlockSpec((None, rows, tw), lambda l, j, p: (l, 0, p * nt + j))],
        out_specs=pl.BlockSpec((None, None, rows, tw), lambda l, j, p: (l, j, 0, p)),
        out_shape=jax.ShapeDtypeStruct((n_l, nt, rows, parts * tw), dtype),
        compiler_params=pltpu.CompilerParams(
            dimension_semantics=("arbitrary", "arbitrary", "arbitrary"), vmem_limit_bytes=VMEM_LIMIT_BYTES),
        name="tile_cast",
    )(a)


def _conv_ffn(x, xb, w_up_t, conv_p, w_down, g, b, *, layer, alpha, seq, tm):
    n, d = x.shape
    nf, tf = w_up_t.shape[1], w_up_t.shape[3] // 2
    nm = n // tm
    taps = conv_p.shape[1] - 1
    halo = SUBLANES
    assert taps - 1 <= halo and seq % tm == 0 and n % tm == 0 and w_down.shape[1] == nf * tf
    kern = functools.partial(_conv_ffn_kernel, alpha=alpha, nf=nf, tiles_per_seq=seq // tm, taps=taps)
    last = nm * nf - 1
    up_tile = lambda s: jnp.minimum(s, last) // nf
    up_col = lambda s: jnp.minimum(s, last) % nf
    conv_col = lambda s: jnp.clip(s - 1, 0, last) % nf
    down_tile = lambda s: jnp.maximum(s - 2, 0) // nf
    down_col = lambda s: jnp.maximum(s - 2, 0) % nf
    return pl.pallas_call(
        kern,
        grid=(nm * nf + 2,),
        in_specs=[
            pl.BlockSpec((tm, d), lambda s: (up_tile(s), 0)),
            pl.BlockSpec((tm, d), lambda s: (down_tile(s), 0)),
            pl.BlockSpec((None, None, d, 2 * tf), lambda s: (layer, up_col(s), 0, 0)),
            pl.BlockSpec((None, taps + 1, tf), lambda s: (layer, 0, conv_col(s))),
            pl.BlockSpec((None, taps + 1, tf), lambda s: (layer, 0, nf + conv_col(s))),
            pl.BlockSpec((None, tf, d), lambda s: (layer, down_col(s), 0)),
            pl.BlockSpec((1, d), lambda s: (0, 0)),
            pl.BlockSpec((1, d), lambda s: (0, 0)),
        ],
        out_specs=[pl.BlockSpec((tm, d), lambda s: (down_tile(s), 0)),
                   pl.BlockSpec((tm, d), lambda s: (down_tile(s), 0))],
        out_shape=[jax.ShapeDtypeStruct((n, d), F32), jax.ShapeDtypeStruct((n, d), BF16)],
        scratch_shapes=[
            pltpu.VMEM((tm, d), F32),
            pltpu.VMEM((halo + tm, 2 * tf), F32),
            pltpu.VMEM((halo + tm, 2 * tf), F32),
            pltpu.VMEM((tm, tf), BF16),
            pltpu.VMEM((tm, tf), BF16),
            pltpu.VMEM((nf, halo, 2 * tf), F32),
        ],
        compiler_params=pltpu.CompilerParams(
            dimension_semantics=("arbitrary",), vmem_limit_bytes=VMEM_LIMIT_BYTES),
        name="conv_ffn",
    )(xb, x, w_up_t, conv_p, conv_p, w_down, g.reshape(1, d), b.reshape(1, d))


def _even_mixer_kernel(x_ref, win_ref, lng_ref, lnb_ref, ws_ref, bst_ref, wp_ref, ps_ref,
                       y_ref, ext_ref, *, tiles_per_seq, d_a):
    i = pl.program_id(0)
    tm = x_ref.shape[0]
    n_heads = ws_ref.shape[0]
    chunk = ws_ref.shape[1]
    n_groups = wp_ref.shape[0]
    grp = wp_ref.shape[1]
    halo = ext_ref.shape[0] - tm

    h = _dot(x_ref[...].astype(BF16), win_ref[...])
    za = jax.nn.gelu(h[:, :2 * d_a])
    u = za[:, :d_a]
    v = _layer_norm(za[:, d_a:], lng_ref[...], lnb_ref[...]).astype(BF16)

    row = lax.broadcasted_iota(jnp.int32, (chunk, chunk), 0)
    col = lax.broadcasted_iota(jnp.int32, (chunk, chunk), 1)
    causal = col <= row
    bst = bst_ref[...]
    for hd in range(n_heads):
        w = jnp.where(causal, ws_ref[hd], 0.0).astype(BF16)
        cs = slice(hd * A_HEAD, (hd + 1) * A_HEAD)
        for c in range(tm // chunk):
            rs = slice(c * chunk, (c + 1) * chunk)
            s = _dot(w, v[rs, cs]) + bst[:, hd:hd + 1]
            y_ref[rs, cs] = (u[rs, cs] * s).astype(BF16)

    xb = h[:, 2 * d_a:]

    @pl.when(i % tiles_per_seq == 0)
    def _():
        ext_ref[0:halo, :] = jnp.zeros((halo, xb.shape[1]), F32)

    @pl.when(i % tiles_per_seq != 0)
    def _():
        ext_ref[0:halo, :] = ext_ref[tm:tm + halo, :]

    ext_ref[halo:halo + tm, :] = xb
    pos = (i % tiles_per_seq) * tm + lax.broadcasted_iota(jnp.int32, (tm, 1), 0)
    posf = (pos + 1).astype(F32)
    for gi in range(n_groups):
        win = B_WINDOWS[gi]
        cs = slice(gi * grp, (gi + 1) * grp)
        wsum = xb[:, cs]
        for k in range(1, win):
            wsum = wsum + ext_ref[halo - k:halo - k + tm, cs]
        cnt = jnp.minimum(posf, float(win))
        p = wsum / cnt - xb[:, cs]
        yb = _dot(p.astype(BF16), wp_ref[gi]) * ps_ref[:, cs]
        y_ref[:, d_a + gi * grp:d_a + (gi + 1) * grp] = yb.astype(BF16)


def _even_mixer(x, w_in, ln_g, ln_b, w_s, b_s, w_pool, pool_scale, *, seq, tm):
    n, d = x.shape
    d_a = ln_g.shape[0]
    d_b = pool_scale.shape[0]
    n_heads, chunk, _ = w_s.shape
    halo = max(B_WINDOWS)
    assert len(B_WINDOWS) == w_pool.shape[0] and n_heads * A_HEAD == d_a
    assert tm % chunk == 0 and seq % tm == 0 and halo % SUBLANES == 0 and halo <= tm
    kern = functools.partial(_even_mixer_kernel, tiles_per_seq=seq // tm, d_a=d_a)
    full = lambda *shape: pl.BlockSpec(shape, lambda i: (0,) * len(shape))
    return pl.pallas_call(
        kern,
        grid=(n // tm,),
        in_specs=[
            pl.BlockSpec((tm, d), lambda i: (i, 0)),
            full(d, 2 * d_a + d_b),
            full(1, d_a),
            full(1, d_a),
            full(*w_s.shape),
            full(chunk, n_heads),
            full(*w_pool.shape),
            full(1, d_b),
        ],
        out_specs=pl.BlockSpec((tm, d_a + d_b), lambda i: (i, 0)),
        out_shape=jax.ShapeDtypeStruct((n, d_a + d_b), BF16),
        scratch_shapes=[pltpu.VMEM((halo + tm, d_b), F32)],
        compiler_params=pltpu.CompilerParams(
            dimension_semantics=("arbitrary",), vmem_limit_bytes=VMEM_LIMIT_BYTES),
        name="even_mixer",
    )(x, w_in, ln_g.reshape(1, d_a), ln_b.reshape(1, d_a), w_s, b_s.T, w_pool,
      pool_scale.reshape(1, d_b))


def _split3(a):
    hi = a.astype(BF16)
    r = a - hi.astype(F32)
    mid = r.astype(BF16)
    lo = (r - mid.astype(F32)).astype(BF16)
    return hi, mid, lo


def _odd_pipe_kernel(xb_ref, w_ref, lbp_ref, ng_ref, y_ref,
                     p0_ref, p1_ref, st_ref, *, nj, tiles_per_seq, layer):
    s = pl.program_id(0)
    tm = xb_ref.shape[0]
    n_parts = 4
    hw = w_ref.shape[1] // n_parts
    heads = hw // C_HEAD
    kt = MXU_DIM
    n_k = xb_ref.shape[1] // kt
    n_chunks = tm // C_CHUNK
    prev = jnp.maximum(s - 1, 0)
    ip = prev // nj
    jp = prev % nj

    @pl.when(s == 0)
    def _():
        p1_ref[...] = jnp.zeros(p1_ref.shape, F32)

    @pl.when(ip % tiles_per_seq == 0)
    def _():
        for hd in range(heads):
            st_ref[jp * heads + hd] = jnp.zeros((C_HEAD, C_HEAD), F32)

    def stage(pw_ref, pr_ref):
        lbp = lbp_ref[...]
        e = jnp.exp(lbp - jnp.max(lbp, axis=0, keepdims=True))
        sm = e / jnp.sum(e, axis=0, keepdims=True)
        lb = jnp.zeros((1, hw), F32)
        for r in range(1, layer + 1):
            lb = lb + sm[r:r + 1, :]
        ng = ng_ref[...]
        row = lax.broadcasted_iota(jnp.int32, (C_CHUNK, C_CHUNK), 0)
        col = lax.broadcasted_iota(jnp.int32, (C_CHUNK, C_CHUNK), 1)
        causal = col <= row
        tri = causal.astype(BF16)

        pieces = [(part, k) for part in range(n_parts) for k in range(n_k)]
        n_pieces = len(pieces)
        n_slots = 3 * n_chunks
        partial = {}
        slot = [0]

        def project():
            t = slot[0]
            slot[0] += 1
            for _ in range((t + 1) * n_pieces // n_slots - t * n_pieces // n_slots):
                part, k = pieces.pop(0)
                ks = slice(k * kt, (k + 1) * kt)
                ps = slice(part * hw, (part + 1) * hw)
                d = _dot(xb_ref[:, ks], w_ref[ks, ps])
                partial[part] = d if k == 0 else partial[part] + d
                if k == n_k - 1:
                    pw_ref[:, ps] = partial.pop(part)

        head_cols = [slice(hd * C_HEAD, (hd + 1) * C_HEAD) for hd in range(heads)]

        def decays(c):
            rs = slice(c * C_CHUNK, (c + 1) * C_CHUNK)
            f = lb + (1.0 - lb) * jax.nn.sigmoid(pr_ref[rs, hw:2 * hw])
            kk = 1.0 - f
            hi, mid, lo = _split3(jnp.log(f))
            bcum = _dot(tri, hi) + _dot(tri, mid) + _dot(tri, lo)
            blast = bcum[C_CHUNK - 1:C_CHUNK, :]
            return dict(
                q_dec=(jax.nn.silu(pr_ref[rs, 0:hw]) * jnp.exp(bcum)).astype(BF16),
                k_dec=(kk * jnp.exp(-bcum)).astype(BF16),
                k_end=(kk * jnp.exp(blast - bcum)).astype(BF16),
                dec=jnp.exp(blast),
                vb=pr_ref[rs, 2 * hw:3 * hw].astype(BF16),
                gate=jax.nn.sigmoid(pr_ref[rs, 3 * hw:4 * hw]))

        def scores(p):
            att = [jnp.where(causal, _dot_nt(p["q_dec"][:, cs], p["k_dec"][:, cs]), 0.0).astype(BF16)
                   for cs in head_cols]
            upd = [_dot_tn(p["vb"][:, cs], p["k_end"][:, cs]) for cs in head_cols]
            return att, upd

        def outputs(c, p, att, upd):
            rs = slice(c * C_CHUNK, (c + 1) * C_CHUNK)
            for hd, cs in enumerate(head_cols):
                st = states[hd]
                o = _dot(att[hd], p["vb"][:, cs]) + _dot_nt(p["q_dec"][:, cs], st.astype(BF16))
                states[hd] = p["dec"][:, cs] * st + upd[hd]
                o = o * lax.rsqrt(jnp.mean(jnp.square(o), axis=-1, keepdims=True) + LN_EPS) * ng[:, cs]
                y_ref[rs, cs] = (o * p["gate"][:, cs]).astype(BF16)

        states = [st_ref[jp * heads + hd] for hd in range(heads)]
        dec_out, sc_out = {}, {}
        for t in range(n_chunks + 2):
            if t < n_chunks:
                project()
                dec_out[t] = decays(t)
            if 0 <= t - 1 < n_chunks:
                project()
                sc_out[t - 1] = scores(dec_out[t - 1])
            if 0 <= t - 2 < n_chunks:
                project()
                outputs(t - 2, dec_out.pop(t - 2), *sc_out.pop(t - 2))
        assert not pieces and not partial
        for hd in range(heads):
            st_ref[jp * heads + hd] = states[hd]

    @pl.when(s % 2 == 0)
    def _():
        stage(p0_ref, p1_ref)

    @pl.when(s % 2 == 1)
    def _():
        stage(p1_ref, p0_ref)


def _odd_pipe(xb, w_in_t, lb_param, norm_g, *, layer, seq, tm):
    n, d = xb.shape
    d_c = norm_g.shape[0]
    depth = lb_param.shape[0]
    nj, hw = w_in_t.shape[0], w_in_t.shape[2] // 4
    nm = n // tm
    assert hw % C_HEAD == 0 and nj * hw == d_c and tm % C_CHUNK == 0 and seq % tm == 0
    assert d % MXU_DIM == 0 and n % tm == 0
    kern = functools.partial(_odd_pipe_kernel, nj=nj, tiles_per_seq=seq // tm, layer=layer)
    last = nm * nj - 1
    cur = lambda s: jnp.minimum(s, last)
    prv = lambda s: jnp.maximum(s - 1, 0)
    return pl.pallas_call(
        kern,
        grid=(nm * nj + 1,),
        in_specs=[
            pl.BlockSpec((tm, d), lambda s: (cur(s) // nj, 0)),
            pl.BlockSpec((None, d, 4 * hw), lambda s: (cur(s) % nj, 0, 0)),
            pl.BlockSpec((depth, hw), lambda s: (0, prv(s) % nj)),
            pl.BlockSpec((1, hw), lambda s: (0, prv(s) % nj)),
        ],
        out_specs=pl.BlockSpec((tm, hw), lambda s: (prv(s) // nj, prv(s) % nj)),
        out_shape=jax.ShapeDtypeStruct((n, d_c), BF16),
        scratch_shapes=[
            pltpu.VMEM((tm, 4 * hw), F32),
            pltpu.VMEM((tm, 4 * hw), F32),
            pltpu.VMEM((d_c // C_HEAD, C_HEAD, C_HEAD), F32),
        ],
        compiler_params=pltpu.CompilerParams(
            dimension_semantics=("arbitrary",), vmem_limit_bytes=VMEM_LIMIT_BYTES),
        name="odd_mixer",
    )(xb, w_in_t, lb_param, norm_g.reshape(1, d_c))


def kernel(x, ev_w_in, ev_ln_v_g, ev_ln_v_b, ev_w_s, ev_b_s, ev_w_pool, ev_pool_scale,
           ev_w_out, od_w_in, od_norm_g, od_w_out, lb_param, ffn_w_up, ffn_conv_w,
           ffn_conv_b, ffn_w_down, ln1_g, ln1_b, ln2_g, ln2_b):
    bn, seq, d = x.shape
    depth = ln1_g.shape[0]
    alpha = (2 * depth) ** 0.25
    xf = x.reshape(bn * seq, d)
    tf = FFN_COL_TILE
    nf = ffn_w_down.shape[1] // tf
    ffn_w_up_t = _tile_columns(ffn_w_up, 2, nf, tf, BF16)
    ffn_conv_p = jnp.concatenate([ffn_conv_w, ffn_conv_b[:, None, :]], axis=1)
    ffn_w_down_b = ffn_w_down.astype(BF16)
    for l in range(depth):
        if l % 2 == 0:
            e = l // 2
            y = _even_mixer(xf, ev_w_in[e].astype(BF16), ev_ln_v_g[e], ev_ln_v_b[e], ev_w_s[e],
                            ev_b_s[e], ev_w_pool[e].astype(BF16), ev_pool_scale[e],
                            seq=seq, tm=min(256, seq))
            w_out = ev_w_out[e]
        else:
            o = l // 2
            hw = ODD_GROUP_WIDTH
            w_in_t = _tile_columns(od_w_in[o:o + 1], 4, od_norm_g.shape[1] // hw, hw, BF16)[0]
            y = _odd_pipe(xb, w_in_t, lb_param, od_norm_g[o], layer=l, seq=seq, tm=min(512, seq))
            w_out = od_w_out[o]
        xf, xb = _proj_ln(y, w_out.astype(BF16), xf, ln1_g[l], ln1_b[l], alpha=alpha, tm=min(512, seq))
        xf, xb = _conv_ffn(xf, xb, ffn_w_up_t, ffn_conv_p, ffn_w_down_b, ln2_g[l], ln2_b[l],
                           layer=l, alpha=alpha, seq=seq, tm=min(512, seq))
    return xf.reshape(bn, seq, d)
```

```python
import functools

import jax
import jax.numpy as jnp
from jax import lax
from jax.experimental import pallas as pl
from jax.experimental.pallas import tpu as pltpu

F32 = jnp.float32
BF16 = jnp.bfloat16

A_HEAD = 128
B_WINDOWS = (2, 4, 8, 16)
C_HEAD = 128
C_CHUNK = 64
LN_EPS = 1e-5

SUBLANES = 8
MXU_DIM = 256
UP_PIECE_K = MXU_DIM
ROW_TILE = 512
EVEN_ROW_TILE = 256
FFN_COL_TILE = 512
ODD_GROUP_WIDTH = 2 * C_HEAD
GATE_ROWS = 16
VMEM_LIMIT_BYTES = 56 * 1024 * 1024


def _dot(a, b):
    return jnp.dot(a, b, preferred_element_type=F32)


def _dot_nt(a, b):
    return lax.dot_general(a, b, (((1,), (1,)), ((), ())), preferred_element_type=F32)


def _dot_tn(a, b):
    return lax.dot_general(a, b, (((0,), (0,)), ((), ())), preferred_element_type=F32)


def _layer_norm(z, g, b):
    mu = jnp.mean(z, axis=-1, keepdims=True)
    zc = z - mu
    var = jnp.mean(jnp.square(zc), axis=-1, keepdims=True)
    return zc * lax.rsqrt(var + LN_EPS) * g + b


def _proj_ln_kernel(y_ref, w_ref, x_ref, g_ref, b_ref, o_ref, ob_ref, *, alpha):
    mix = _dot(y_ref[...], w_ref[...])
    out = _layer_norm(alpha * x_ref[...] + mix, g_ref[...], b_ref[...])
    o_ref[...] = out
    ob_ref[...] = out.astype(BF16)


def _proj_ln(y, w, x, g, b, *, alpha, tm):
    n, d = x.shape
    k = y.shape[1]
    return pl.pallas_call(
        functools.partial(_proj_ln_kernel, alpha=alpha),
        grid=(n // tm,),
        in_specs=[
            pl.BlockSpec((tm, k), lambda i: (i, 0)),
            pl.BlockSpec((k, d), lambda i: (0, 0)),
            pl.BlockSpec((tm, d), lambda i: (i, 0)),
            pl.BlockSpec((1, d), lambda i: (0, 0)),
            pl.BlockSpec((1, d), lambda i: (0, 0)),
        ],
        out_specs=[pl.BlockSpec((tm, d), lambda i: (i, 0)), pl.BlockSpec((tm, d), lambda i: (i, 0))],
        out_shape=[jax.ShapeDtypeStruct((n, d), F32), jax.ShapeDtypeStruct((n, d), BF16)],
        compiler_params=pltpu.CompilerParams(
            dimension_semantics=("arbitrary",), vmem_limit_bytes=VMEM_LIMIT_BYTES),
        name="proj_ln",
    )(y, w, x, g.reshape(1, d), b.reshape(1, d))


def _conv_ffn_kernel(xb_ref, xr_ref, wu_ref, cpa_ref, cpv_ref, wd_ref,
                     g_ref, b_ref, o_ref, ob_ref, acc_ref, h0_ref, h1_ref, g0_ref, g1_ref, carry_ref,
                     *, alpha, nf, tiles_per_seq, taps):
    s = pl.program_id(0)
    tm = xb_ref.shape[0]
    tf = wu_ref.shape[1] // 2
    halo = SUBLANES
    i = s // nf
    j = s % nf

    @pl.when(s == 0)
    def _():
        h1_ref[...] = jnp.zeros(h1_ref.shape, F32)
        g0_ref[...] = jnp.zeros(g0_ref.shape, BF16)
        acc_ref[...] = jnp.zeros(acc_ref.shape, F32)
        carry_ref[...] = jnp.zeros(carry_ref.shape, F32)

    def stage(hw_ref, hr_ref, gw_ref, gr_ref):
        seq_start = (i % tiles_per_seq) == 0
        d_in = xb_ref.shape[1]
        kt = UP_PIECE_K
        n_k = d_in // kt
        rb = GATE_ROWS
        per_piece = tm // rb // (2 * n_k)
        d_out = acc_ref.shape[1]

        def gate(blk):
            r0 = blk * rb

            def conv(cs, cp_ref):
                cp = cp_ref[...]
                hc = cp[taps:taps + 1, :] + cp[taps - 1:taps, :] * hr_ref[halo + r0:halo + r0 + rb, cs]
                for tap in range(taps - 1):
                    sh = taps - 1 - tap
                    hc = hc + cp[tap:tap + 1, :] * hr_ref[halo + r0 - sh:halo + r0 - sh + rb, cs]
                return hc

            a = conv(slice(0, tf), cpa_ref)
            v = conv(slice(tf, 2 * tf), cpv_ref)
            gw_ref[r0:r0 + rb, :] = (jax.nn.silu(a) * v).astype(BF16)

        down_blocks = list(range(0, d_out, MXU_DIM))
        n_down = len(down_blocks)
        n_up = 2 * n_k

        def down(n_blocks):
            for _ in range(n_blocks):
                n0 = down_blocks.pop(0)
                ns = slice(n0, n0 + MXU_DIM)
                acc_ref[:, ns] += _dot(gr_ref[...], wd_ref[:, ns])

        blk = 0
        piece = 0
        for half in range(2):
            cs = slice(half * tf, (half + 1) * tf)
            h = None
            for k in range(n_k):
                ks = slice(k * kt, (k + 1) * kt)
                part = _dot(xb_ref[:, ks], wu_ref[ks, cs])
                h = part if h is None else h + part
                for _ in range(per_piece):
                    gate(blk)
                    blk += 1
                piece += 1
                down(piece * n_down // n_up - (piece - 1) * n_down // n_up)
            hw_ref[0:halo, cs] = jnp.where(seq_start, 0.0, carry_ref[j, :, cs])
            hw_ref[halo:halo + tm, cs] = h
            carry_ref[j, :, cs] = h[tm - halo:, :]
        assert not down_blocks

    @pl.when(s % 2 == 0)
    def _():
        stage(h0_ref, h1_ref, g1_ref, g0_ref)

    @pl.when(s % 2 == 1)
    def _():
        stage(h1_ref, h0_ref, g0_ref, g1_ref)

    @pl.when(s < 2)
    def _():
        acc_ref[...] = jnp.zeros(acc_ref.shape, F32)

    @pl.when((s >= 2) & ((s - 2) % nf == nf - 1))
    def _():
        out = _layer_norm(alpha * xr_ref[...] + acc_ref[...], g_ref[...], b_ref[...])
        o_ref[...] = out
        ob_ref[...] = out.astype(BF16)
        acc_ref[...] = jnp.zeros(acc_ref.shape, F32)


def _tile_cast_kernel(a_ref, o_ref):
    o_ref[...] = a_ref[...].astype(o_ref.dtype)


def _tile_columns(a, parts, nt, tw, dtype):
    n_l, rows, cols = a.shape
    assert cols == parts * nt * tw
    return pl.pallas_call(
        _tile_cast_kernel,
        grid=(n_l, nt, parts),
        in_specs=[pl.BlockSpec((None, rows, tw), lambda l, j, p: (l, 0, p * nt + j))],
        out_specs=pl.BlockSpec((None, None, rows, tw), lambda l, j, p: (l, j, 0, p)),
        out_shape=jax.ShapeDtypeStruct((n_l, nt, rows, parts * tw), dtype),
        compiler_params=pltpu.CompilerParams(
            dimension_semantics=("arbitrary", "arbitrary", "arbitrary"), vmem_limit_bytes=VMEM_LIMIT_BYTES),
        name="tile_cast",
    )(a)


def _conv_ffn(x, xb, w_up_t, conv_p, w_down, g, b, *, layer, alpha, seq, tm):
    n, d = x.shape
    nf, tf = w_up_t.shape[1], w_up_t.shape[3] // 2
    nm = n // tm
    taps = conv_p.shape[1] - 1
    halo = SUBLANES
    assert taps - 1 <= halo and seq % tm == 0 and n % tm == 0 and w_down.shape[1] == nf * tf
    kern = functools.partial(_conv_ffn_kernel, alpha=alpha, nf=nf, tiles_per_seq=seq // tm, taps=taps)
    last = nm * nf - 1
    up_tile = lambda s: jnp.minimum(s, last) // nf
    up_col = lambda s: jnp.minimum(s, last) % nf
    conv_col = lambda s: jnp.clip(s - 1, 0, last) % nf
    down_tile = lambda s: jnp.maximum(s - 2, 0) // nf
    down_col = lambda s: jnp.maximum(s - 2, 0) % nf
    return pl.pallas_call(
        kern,
        grid=(nm * nf + 2,),
        in_specs=[
            pl.BlockSpec((tm, d), lambda s: (up_tile(s), 0)),
            pl.BlockSpec((tm, d), lambda s: (down_tile(s), 0)),
            pl.BlockSpec((None, None, d, 2 * tf), lambda s: (layer, up_col(s), 0, 0)),
            pl.BlockSpec((None, taps + 1, tf), lambda s: (layer, 0, conv_col(s))),
            pl.BlockSpec((None, taps + 1, tf), lambda s: (layer, 0, nf + conv_col(s))),
            pl.BlockSpec((None, tf, d), lambda s: (layer, down_col(s), 0)),
            pl.BlockSpec((1, d), lambda s: (0, 0)),
            pl.BlockSpec((1, d), lambda s: (0, 0)),
        ],
        out_specs=[pl.BlockSpec((tm, d), lambda s: (down_tile(s), 0)),
                   pl.BlockSpec((tm, d), lambda s: (down_tile(s), 0))],
        out_shape=[jax.ShapeDtypeStruct((n, d), F32), jax.ShapeDtypeStruct((n, d), BF16)],
        scratch_shapes=[
            pltpu.VMEM((tm, d), F32),
            pltpu.VMEM((halo + tm, 2 * tf), F32),
            pltpu.VMEM((halo + tm, 2 * tf), F32),
            pltpu.VMEM((tm, tf), BF16),
            pltpu.VMEM((tm, tf), BF16),
            pltpu.VMEM((nf, halo, 2 * tf), F32),
        ],
        compiler_params=pltpu.CompilerParams(
            dimension_semantics=("arbitrary",), vmem_limit_bytes=VMEM_LIMIT_BYTES),
        name="conv_ffn",
    )(xb, x, w_up_t, conv_p, conv_p, w_down, g.reshape(1, d), b.reshape(1, d))


def _even_mixer_kernel(x_ref, win_ref, lng_ref, lnb_ref, ws_ref, bst_ref, wp_ref, ps_ref,
                       y_ref, ext_ref, *, tiles_per_seq, d_a):
    i = pl.program_id(0)
    tm = x_ref.shape[0]
    n_heads = ws_ref.shape[0]
    chunk = ws_ref.shape[1]
    n_groups = wp_ref.shape[0]
    grp = wp_ref.shape[1]
    halo = ext_ref.shape[0] - tm

    h = _dot(x_ref[...].astype(BF16), win_ref[...])
    za = jax.nn.gelu(h[:, :2 * d_a])
    u = za[:, :d_a]
    v = _layer_norm(za[:, d_a:], lng_ref[...], lnb_ref[...]).astype(BF16)

    row = lax.broadcasted_iota(jnp.int32, (chunk, chunk), 0)
    col = lax.broadcasted_iota(jnp.int32, (chunk, chunk), 1)
    causal = col <= row
    bst = bst_ref[...]
    for hd in range(n_heads):
        w = jnp.where(causal, ws_ref[hd], 0.0).astype(BF16)
        cs = slice(hd * A_HEAD, (hd + 1) * A_HEAD)
        for c in range(tm // chunk):
            rs = slice(c * chunk, (c + 1) * chunk)
            s = _dot(w, v[rs, cs]) + bst[:, hd:hd + 1]
            y_ref[rs, cs] = (u[rs, cs] * s).astype(BF16)

    xb = h[:, 2 * d_a:]

    @pl.when(i % tiles_per_seq == 0)
    def _():
        ext_ref[0:halo, :] = jnp.zeros((halo, xb.shape[1]), F32)

    @pl.when(i % tiles_per_seq != 0)
    def _():
        ext_ref[0:halo, :] = ext_ref[tm:tm + halo, :]

    ext_ref[halo:halo + tm, :] = xb
    pos = (i % tiles_per_seq) * tm + lax.broadcasted_iota(jnp.int32, (tm, 1), 0)
    posf = (pos + 1).astype(F32)
    for gi in range(n_groups):
        win = B_WINDOWS[gi]
        cs = slice(gi * grp, (gi + 1) * grp)
        wsum = xb[:, cs]
        for k in range(1, win):
            wsum = wsum + ext_ref[halo - k:halo - k + tm, cs]
        cnt = jnp.minimum(posf, float(win))
        p = wsum / cnt - xb[:, cs]
        yb = _dot(p.astype(BF16), wp_ref[gi]) * ps_ref[:, cs]
        y_ref[:, d_a + gi * grp:d_a + (gi + 1) * grp] = yb.astype(BF16)


def _even_mixer(x, w_in, ln_g, ln_b, w_s, b_s, w_pool, pool_scale, *, seq, tm):
    n, d = x.shape
    d_a = ln_g.shape[0]
    d_b = pool_scale.shape[0]
    n_heads, chunk, _ = w_s.shape
    halo = max(B_WINDOWS)
    assert len(B_WINDOWS) == w_pool.shape[0] and n_heads * A_HEAD == d_a
    assert tm % chunk == 0 and seq % tm == 0 and halo % SUBLANES == 0 and halo <= tm
    kern = functools.partial(_even_mixer_kernel, tiles_per_seq=seq // tm, d_a=d_a)
    full = lambda *shape: pl.BlockSpec(shape, lambda i: (0,) * len(shape))
    return pl.pallas_call(
        kern,
        grid=(n // tm,),
        in_specs=[
            pl.BlockSpec((tm, d), lambda i: (i, 0)),
            full(d, 2 * d_a + d_b),
            full(1, d_a),
            full(1, d_a),
            full(*w_s.shape),
            full(chunk, n_heads),
            full(*w_pool.shape),
            full(1, d_b),
        ],
        out_specs=pl.BlockSpec((tm, d_a + d_b), lambda i: (i, 0)),
        out_shape=jax.ShapeDtypeStruct((n, d_a + d_b), BF16),
        scratch_shapes=[pltpu.VMEM((halo + tm, d_b), F32)],
        compiler_params=pltpu.CompilerParams(
            dimension_semantics=("arbitrary",), vmem_limit_bytes=VMEM_LIMIT_BYTES),
        name="even_mixer",
    )(x, w_in, ln_g.reshape(1, d_a), ln_b.reshape(1, d_a), w_s, b_s.T, w_pool,
      pool_scale.reshape(1, d_b))


def _split3(a):
    hi = a.astype(BF16)
    r = a - hi.astype(F32)
    mid = r.astype(BF16)
    lo = (r - mid.astype(F32)).astype(BF16)
    return hi, mid, lo


def _odd_pipe_kernel(xb_ref, w_ref, lbng_ref, y_ref,
                     p0_ref, p1_ref, st_ref, *, nj, tiles_per_seq, layer):
    s = pl.program_id(0)
    tm = xb_ref.shape[0]
    n_parts = 4
    hw = w_ref.shape[1] // n_parts
    heads = hw // C_HEAD
    kt = MXU_DIM
    n_k = xb_ref.shape[1] // kt
    n_chunks = tm // C_CHUNK
    prev = jnp.maximum(s - 1, 0)
    ip = prev // nj
    jp = prev % nj

    @pl.when(s == 0)
    def _():
        p1_ref[...] = jnp.zeros(p1_ref.shape, F32)

    @pl.when(ip % tiles_per_seq == 0)
    def _():
        for hd in range(heads):
            st_ref[jp * heads + hd] = jnp.zeros((C_HEAD, C_HEAD), F32)

    def stage(pw_ref, pr_ref):
        depth = lbng_ref.shape[0] - 1
        lbp = lbng_ref[0:depth, :]
        e = jnp.exp(lbp - jnp.max(lbp, axis=0, keepdims=True))
        sm = e / jnp.sum(e, axis=0, keepdims=True)
        lb = jnp.zeros((1, hw), F32)
        for r in range(1, layer + 1):
            lb = lb + sm[r:r + 1, :]
        ng = lbng_ref[depth:depth + 1, :]
        row = lax.broadcasted_iota(jnp.int32, (C_CHUNK, C_CHUNK), 0)
        col = lax.broadcasted_iota(jnp.int32, (C_CHUNK, C_CHUNK), 1)
        causal = col <= row
        tri = causal.astype(BF16)

        pieces = [(part, k) for part in range(n_parts) for k in range(n_k)]
        n_pieces = len(pieces)
        n_slots = 3 * n_chunks
        partial = {}
        slot = [0]

        def project():
            t = slot[0]
            slot[0] += 1
            for _ in range((t + 1) * n_pieces // n_slots - t * n_pieces // n_slots):
                part, k = pieces.pop(0)
                ks = slice(k * kt, (k + 1) * kt)
                ps = slice(part * hw, (part + 1) * hw)
                d = _dot(xb_ref[:, ks], w_ref[ks, ps])
                partial[part] = d if k == 0 else partial[part] + d
                if k == n_k - 1:
                    pw_ref[:, ps] = partial.pop(part)

        head_cols = [slice(hd * C_HEAD, (hd + 1) * C_HEAD) for hd in range(heads)]

        def decays(c):
            rs = slice(c * C_CHUNK, (c + 1) * C_CHUNK)
            f = lb + (1.0 - lb) * jax.nn.sigmoid(pr_ref[rs, hw:2 * hw])
            kk = 1.0 - f
            hi, mid, lo = _split3(jnp.log(f))
            bcum = _dot(tri, hi) + _dot(tri, mid) + _dot(tri, lo)
            blast = bcum[C_CHUNK - 1:C_CHUNK, :]
            return dict(
                q_dec=(jax.nn.silu(pr_ref[rs, 0:hw]) * jnp.exp(bcum)).astype(BF16),
                k_dec=(kk * jnp.exp(-bcum)).astype(BF16),
                k_end=(kk * jnp.exp(blast - bcum)).astype(BF16),
                dec=jnp.exp(blast),
                vb=pr_ref[rs, 2 * hw:3 * hw].astype(BF16),
                gate=jax.nn.sigmoid(pr_ref[rs, 3 * hw:4 * hw]))

        def scores(p):
            att = [jnp.where(causal, _dot_nt(p["q_dec"][:, cs], p["k_dec"][:, cs]), 0.0).astype(BF16)
                   for cs in head_cols]
            upd = [_dot_tn(p["vb"][:, cs], p["k_end"][:, cs]) for cs in head_cols]
            return att, upd

        def outputs(c, p, att, upd):
            rs = slice(c * C_CHUNK, (c + 1) * C_CHUNK)
            for hd, cs in enumerate(head_cols):
                st = states[hd]
                o = _dot(att[hd], p["vb"][:, cs]) + _dot_nt(p["q_dec"][:, cs], st.astype(BF16))
                states[hd] = p["dec"][:, cs] * st + upd[hd]
                o = o * lax.rsqrt(jnp.mean(jnp.square(o), axis=-1, keepdims=True) + LN_EPS) * ng[:, cs]
                y_ref[rs, cs] = (o * p["gate"][:, cs]).astype(BF16)

        states = [st_ref[jp * heads + hd] for hd in range(heads)]
        dec_out, sc_out = {}, {}
        for t in range(n_chunks + 2):
            if t < n_chunks:
                project()
                dec_out[t] = decays(t)
            if 0 <= t - 1 < n_chunks:
                project()
                sc_out[t - 1] = scores(dec_out[t - 1])
            if 0 <= t - 2 < n_chunks:
                project()
                outputs(t - 2, dec_out.pop(t - 2), *sc_out.pop(t - 2))
        assert not pieces and not partial
        for hd in range(heads):
            st_ref[jp * heads + hd] = states[hd]

    @pl.when(s % 2 == 0)
    def _():
        stage(p0_ref, p1_ref)

    @pl.when(s % 2 == 1)
    def _():
        stage(p1_ref, p0_ref)


def _odd_pipe(xb, w_in_t, lb_param, norm_g, *, layer, seq, tm):
    n, d = xb.shape
    d_c = norm_g.shape[0]
    depth = lb_param.shape[0]
    nj, hw = w_in_t.shape[0], w_in_t.shape[2] // 4
    nm = n // tm
    assert hw % C_HEAD == 0 and nj * hw == d_c and tm % C_CHUNK == 0 and seq % tm == 0
    assert d % MXU_DIM == 0 and n % tm == 0
    kern = functools.partial(_odd_pipe_kernel, nj=nj, tiles_per_seq=seq // tm, layer=layer)
    last = nm * nj - 1
    cur = lambda s: jnp.minimum(s, last)
    prv = lambda s: jnp.maximum(s - 1, 0)
    return pl.pallas_call(
        kern,
        grid=(nm * nj + 1,),
        in_specs=[
            pl.BlockSpec((tm, d), lambda s: (cur(s) // nj, 0)),
            pl.BlockSpec((None, d, 4 * hw), lambda s: (cur(s) % nj, 0, 0)),
            pl.BlockSpec((depth + 1, hw), lambda s: (0, prv(s) % nj)),
        ],
        out_specs=pl.BlockSpec((tm, hw), lambda s: (prv(s) // nj, prv(s) % nj)),
        out_shape=jax.ShapeDtypeStruct((n, d_c), BF16),
        scratch_shapes=[
            pltpu.VMEM((tm, 4 * hw), F32),
            pltpu.VMEM((tm, 4 * hw), F32),
            pltpu.VMEM((d_c // C_HEAD, C_HEAD, C_HEAD), F32),
        ],
        compiler_params=pltpu.CompilerParams(
            dimension_semantics=("arbitrary",), vmem_limit_bytes=VMEM_LIMIT_BYTES),
        name="odd_mixer",
    )(xb, w_in_t, jnp.concatenate([lb_param, norm_g.reshape(1, d_c)], axis=0))


def kernel(x, ev_w_in, ev_ln_v_g, ev_ln_v_b, ev_w_s, ev_b_s, ev_w_pool, ev_pool_scale,
           ev_w_out, od_w_in, od_norm_g, od_w_out, lb_param, ffn_w_up, ffn_conv_w,
           ffn_conv_b, ffn_w_down, ln1_g, ln1_b, ln2_g, ln2_b):
    bn, seq, d = x.shape
    depth = ln1_g.shape[0]
    alpha = (2 * depth) ** 0.25
    xf = x.reshape(bn * seq, d)
    tf = FFN_COL_TILE
    nf = ffn_w_down.shape[1] // tf
    ffn_w_up_t = _tile_columns(ffn_w_up, 2, nf, tf, BF16)
    ffn_conv_p = jnp.concatenate([ffn_conv_w, ffn_conv_b[:, None, :]], axis=1)
    ffn_w_down_b = ffn_w_down.astype(BF16)
    for l in range(depth):
        if l % 2 == 0:
            e = l // 2
            y = _even_mixer(xf, ev_w_in[e].astype(BF16), ev_ln_v_g[e], ev_ln_v_b[e], ev_w_s[e],
                            ev_b_s[e], ev_w_pool[e].astype(BF16), ev_pool_scale[e],
                            seq=seq, tm=min(EVEN_ROW_TILE, seq))
            w_out = ev_w_out[e]
        else:
            o = l // 2
            hw = ODD_GROUP_WIDTH
            w_in_t = _tile_columns(od_w_in[o:o + 1], 4, od_norm_g.shape[1] // hw, hw, BF16)[0]
            y = _odd_pipe(xb, w_in_t, lb_param, od_norm_g[o], layer=l, seq=seq, tm=min(ROW_TILE, seq))
            w_out = od_w_out[o]
        xf, xb = _proj_ln(y, w_out.astype(BF16), xf, ln1_g[l], ln1_b[l], alpha=alpha,
                          tm=min(ROW_TILE, seq))
        xf, xb = _conv_ffn(xf, xb, ffn_w_up_t, ffn_conv_p, ffn_w_down_b, ln2_g[l], ln2_b[l],
                           layer=l, alpha=alpha, seq=seq, tm=min(ROW_TILE, seq))
    return xf.reshape(bn, seq, d)
```

```python
import functools

import jax
import jax.numpy as jnp
from jax import lax
from jax.experimental import pallas as pl
from jax.experimental.pallas import tpu as pltpu

F32 = jnp.float32
BF16 = jnp.bfloat16

A_HEAD = 128
B_WINDOWS = (2, 4, 8, 16)
C_HEAD = 128
C_CHUNK = 64
LN_EPS = 1e-5

SUBLANES = 8
MXU_DIM = 256
UP_PIECE_K = MXU_DIM
ROW_TILE = 512
EVEN_ROW_TILE = 256
FFN_COL_TILE = 512
ODD_GROUP_WIDTH = 2 * C_HEAD
GATE_ROWS = 16
VMEM_LIMIT_BYTES = 56 * 1024 * 1024


def _dot(a, b):
    return jnp.dot(a, b, preferred_element_type=F32)


def _dot_nt(a, b):
    return lax.dot_general(a, b, (((1,), (1,)), ((), ())), preferred_element_type=F32)


def _dot_tn(a, b):
    return lax.dot_general(a, b, (((0,), (0,)), ((), ())), preferred_element_type=F32)


def _layer_norm(z, g, b):
    mu = jnp.mean(z, axis=-1, keepdims=True)
    zc = z - mu
    var = jnp.mean(jnp.square(zc), axis=-1, keepdims=True)
    return zc * lax.rsqrt(var + LN_EPS) * g + b


def _proj_ln_kernel(y_ref, w_ref, x_ref, g_ref, b_ref, o_ref, ob_ref, *, alpha):
    mix = _dot(y_ref[...], w_ref[...])
    out = _layer_norm(alpha * x_ref[...] + mix, g_ref[...], b_ref[...])
    o_ref[...] = out
    ob_ref[...] = out.astype(BF16)


def _proj_ln(y, w, x, g, b, *, alpha, tm):
    n, d = x.shape
    k = y.shape[1]
    return pl.pallas_call(
        functools.partial(_proj_ln_kernel, alpha=alpha),
        grid=(n // tm,),
        in_specs=[
            pl.BlockSpec((tm, k), lambda i: (i, 0)),
            pl.BlockSpec((k, d), lambda i: (0, 0)),
            pl.BlockSpec((tm, d), lambda i: (i, 0)),
            pl.BlockSpec((1, d), lambda i: (0, 0)),
            pl.BlockSpec((1, d), lambda i: (0, 0)),
        ],
        out_specs=[pl.BlockSpec((tm, d), lambda i: (i, 0)), pl.BlockSpec((tm, d), lambda i: (i, 0))],
        out_shape=[jax.ShapeDtypeStruct((n, d), F32), jax.ShapeDtypeStruct((n, d), BF16)],
        compiler_params=pltpu.CompilerParams(
            dimension_semantics=("arbitrary",), vmem_limit_bytes=VMEM_LIMIT_BYTES),
        name="proj_ln",
    )(y, w, x, g.reshape(1, d), b.reshape(1, d))


def _conv_ffn_kernel(xb_ref, xr_ref, wu_ref, cp_ref, wd_ref,
                     g_ref, b_ref, o_ref, ob_ref, acc_ref, h0_ref, h1_ref, g0_ref, g1_ref, carry_ref,
                     *, alpha, nf, tiles_per_seq, taps):
    s = pl.program_id(0)
    tm = xb_ref.shape[0]
    tf = wu_ref.shape[1] // 2
    halo = SUBLANES
    i = s // nf
    j = s % nf

    @pl.when(s == 0)
    def _():
        h1_ref[...] = jnp.zeros(h1_ref.shape, F32)
        g0_ref[...] = jnp.zeros(g0_ref.shape, BF16)
        acc_ref[...] = jnp.zeros(acc_ref.shape, F32)
        carry_ref[...] = jnp.zeros(carry_ref.shape, F32)

    def stage(hw_ref, hr_ref, gw_ref, gr_ref):
        seq_start = (i % tiles_per_seq) == 0
        d_in = xb_ref.shape[1]
        kt = UP_PIECE_K
        n_k = d_in // kt
        rb = GATE_ROWS
        per_piece = tm // rb // (2 * n_k)
        d_out = acc_ref.shape[1]

        def gate(blk):
            r0 = blk * rb

            def conv(cs):
                cp = cp_ref[:, cs]
                hc = cp[taps:taps + 1, :] + cp[taps - 1:taps, :] * hr_ref[halo + r0:halo + r0 + rb, cs]
                for tap in range(taps - 1):
                    sh = taps - 1 - tap
                    hc = hc + cp[tap:tap + 1, :] * hr_ref[halo + r0 - sh:halo + r0 - sh + rb, cs]
                return hc

            a = conv(slice(0, tf))
            v = conv(slice(tf, 2 * tf))
            gw_ref[r0:r0 + rb, :] = (jax.nn.silu(a) * v).astype(BF16)

        down_blocks = list(range(0, d_out, MXU_DIM))
        n_down = len(down_blocks)
        n_up = 2 * n_k

        def down(n_blocks):
            for _ in range(n_blocks):
                n0 = down_blocks.pop(0)
                ns = slice(n0, n0 + MXU_DIM)
                acc_ref[:, ns] += _dot(gr_ref[...], wd_ref[:, ns])

        blk = 0
        piece = 0
        for half in range(2):
            cs = slice(half * tf, (half + 1) * tf)
            h = None
            for k in range(n_k):
                ks = slice(k * kt, (k + 1) * kt)
                part = _dot(xb_ref[:, ks], wu_ref[ks, cs])
                h = part if h is None else h + part
                for _ in range(per_piece):
                    gate(blk)
                    blk += 1
                piece += 1
                down(piece * n_down // n_up - (piece - 1) * n_down // n_up)
            hw_ref[0:halo, cs] = jnp.where(seq_start, 0.0, carry_ref[j, :, cs])
            hw_ref[halo:halo + tm, cs] = h
            carry_ref[j, :, cs] = h[tm - halo:, :]
        assert not down_blocks

    @pl.when(s % 2 == 0)
    def _():
        stage(h0_ref, h1_ref, g1_ref, g0_ref)

    @pl.when(s % 2 == 1)
    def _():
        stage(h1_ref, h0_ref, g0_ref, g1_ref)

    @pl.when(s < 2)
    def _():
        acc_ref[...] = jnp.zeros(acc_ref.shape, F32)

    @pl.when((s >= 2) & ((s - 2) % nf == nf - 1))
    def _():
        out = _layer_norm(alpha * xr_ref[...] + acc_ref[...], g_ref[...], b_ref[...])
        o_ref[...] = out
        ob_ref[...] = out.astype(BF16)
        acc_ref[...] = jnp.zeros(acc_ref.shape, F32)


def _tile_cast_kernel(a_ref, o_ref):
    o_ref[...] = a_ref[...].astype(o_ref.dtype)


def _tile_columns(a, parts, nt, tw, dtype):
    n_l, rows, cols = a.shape
    assert cols == parts * nt * tw
    return pl.pallas_call(
        _tile_cast_kernel,
        grid=(n_l, nt, parts),
        in_specs=[pl.BlockSpec((None, rows, tw), lambda l, j, p: (l, 0, p * nt + j))],
        out_specs=pl.BlockSpec((None, None, rows, tw), lambda l, j, p: (l, j, 0, p)),
        out_shape=jax.ShapeDtypeStruct((n_l, nt, rows, parts * tw), dtype),
        compiler_params=pltpu.CompilerParams(
            dimension_semantics=("arbitrary", "arbitrary", "arbitrary"), vmem_limit_bytes=VMEM_LIMIT_BYTES),
        name="tile_cast",
    )(a)


def _tile_rows_kernel(a_ref, o_ref, *, parts, nt, tw):
    for j in range(nt):
        for p in range(parts):
            c0 = (p * nt + j) * tw
            o_ref[j, :, p * tw:(p + 1) * tw] = a_ref[:, c0:c0 + tw]


def _tile_columns_small(a, parts, nt, tw):
    n_l, rows, cols = a.shape
    assert cols == parts * nt * tw
    return pl.pallas_call(
        functools.partial(_tile_rows_kernel, parts=parts, nt=nt, tw=tw),
        grid=(n_l,),
        in_specs=[pl.BlockSpec((None, rows, cols), lambda l: (l, 0, 0))],
        out_specs=pl.BlockSpec((None, nt, rows, parts * tw), lambda l: (l, 0, 0, 0)),
        out_shape=jax.ShapeDtypeStruct((n_l, nt, rows, parts * tw), a.dtype),
        compiler_params=pltpu.CompilerParams(
            dimension_semantics=("arbitrary",), vmem_limit_bytes=VMEM_LIMIT_BYTES),
        name="tile_rows",
    )(a)


def _conv_ffn(x, xb, w_up_t, conv_p_t, w_down, g, b, *, layer, alpha, seq, tm):
    n, d = x.shape
    nf, tf = w_up_t.shape[1], w_up_t.shape[3] // 2
    nm = n // tm
    taps = conv_p_t.shape[2] - 1
    halo = SUBLANES
    assert taps - 1 <= halo and seq % tm == 0 and n % tm == 0 and w_down.shape[1] == nf * tf
    kern = functools.partial(_conv_ffn_kernel, alpha=alpha, nf=nf, tiles_per_seq=seq // tm, taps=taps)
    last = nm * nf - 1
    up_tile = lambda s: jnp.minimum(s, last) // nf
    up_col = lambda s: jnp.minimum(s, last) % nf
    conv_col = lambda s: jnp.clip(s - 1, 0, last) % nf
    down_tile = lambda s: jnp.maximum(s - 2, 0) // nf
    down_col = lambda s: jnp.maximum(s - 2, 0) % nf
    return pl.pallas_call(
        kern,
        grid=(nm * nf + 2,),
        in_specs=[
            pl.BlockSpec((tm, d), lambda s: (up_tile(s), 0)),
            pl.BlockSpec((tm, d), lambda s: (down_tile(s), 0)),
            pl.BlockSpec((None, None, d, 2 * tf), lambda s: (layer, up_col(s), 0, 0)),
            pl.BlockSpec((None, None, taps + 1, 2 * tf), lambda s: (layer, conv_col(s), 0, 0)),
            pl.BlockSpec((None, tf, d), lambda s: (layer, down_col(s), 0)),
            pl.BlockSpec((1, d), lambda s: (0, 0)),
            pl.BlockSpec((1, d), lambda s: (0, 0)),
        ],
        out_specs=[pl.BlockSpec((tm, d), lambda s: (down_tile(s), 0)),
                   pl.BlockSpec((tm, d), lambda s: (down_tile(s), 0))],
        out_shape=[jax.ShapeDtypeStruct((n, d), F32), jax.ShapeDtypeStruct((n, d), BF16)],
        scratch_shapes=[
            pltpu.VMEM((tm, d), F32),
            pltpu.VMEM((halo + tm, 2 * tf), F32),
            pltpu.VMEM((halo + tm, 2 * tf), F32),
            pltpu.VMEM((tm, tf), BF16),
            pltpu.VMEM((tm, tf), BF16),
            pltpu.VMEM((nf, halo, 2 * tf), F32),
        ],
        compiler_params=pltpu.CompilerParams(
            dimension_semantics=("arbitrary",), vmem_limit_bytes=VMEM_LIMIT_BYTES),
        name="conv_ffn",
    )(xb, x, w_up_t, conv_p_t, w_down, g.reshape(1, d), b.reshape(1, d))


def _even_mixer_kernel(x_ref, win_ref, lng_ref, lnb_ref, ws_ref, bst_ref, wp_ref, ps_ref,
                       y_ref, ext_ref, *, tiles_per_seq, d_a):
    i = pl.program_id(0)
    tm = x_ref.shape[0]
    n_heads = ws_ref.shape[0]
    chunk = ws_ref.shape[1]
    n_groups = wp_ref.shape[0]
    grp = wp_ref.shape[1]
    halo = ext_ref.shape[0] - tm

    h = _dot(x_ref[...].astype(BF16), win_ref[...])
    za = jax.nn.gelu(h[:, :2 * d_a])
    u = za[:, :d_a]
    v = _layer_norm(za[:, d_a:], lng_ref[...], lnb_ref[...]).astype(BF16)

    row = lax.broadcasted_iota(jnp.int32, (chunk, chunk), 0)
    col = lax.broadcasted_iota(jnp.int32, (chunk, chunk), 1)
    causal = col <= row
    bst = bst_ref[...]
    for hd in range(n_heads):
        w = jnp.where(causal, ws_ref[hd], 0.0).astype(BF16)
        cs = slice(hd * A_HEAD, (hd + 1) * A_HEAD)
        for c in range(tm // chunk):
            rs = slice(c * chunk, (c + 1) * chunk)
            s = _dot(w, v[rs, cs]) + bst[:, hd:hd + 1]
            y_ref[rs, cs] = (u[rs, cs] * s).astype(BF16)

    xb = h[:, 2 * d_a:]

    @pl.when(i % tiles_per_seq == 0)
    def _():
        ext_ref[0:halo, :] = jnp.zeros((halo, xb.shape[1]), F32)

    @pl.when(i % tiles_per_seq != 0)
    def _():
        ext_ref[0:halo, :] = ext_ref[tm:tm + halo, :]

    ext_ref[halo:halo + tm, :] = xb
    pos = (i % tiles_per_seq) * tm + lax.broadcasted_iota(jnp.int32, (tm, 1), 0)
    posf = (pos + 1).astype(F32)
    for gi in range(n_groups):
        win = B_WINDOWS[gi]
        cs = slice(gi * grp, (gi + 1) * grp)
        wsum = xb[:, cs]
        for k in range(1, win):
            wsum = wsum + ext_ref[halo - k:halo - k + tm, cs]
        cnt = jnp.minimum(posf, float(win))
        p = wsum / cnt - xb[:, cs]
        yb = _dot(p.astype(BF16), wp_ref[gi]) * ps_ref[:, cs]
        y_ref[:, d_a + gi * grp:d_a + (gi + 1) * grp] = yb.astype(BF16)


def _even_mixer(x, w_in, ln_g, ln_b, w_s, b_s, w_pool, pool_scale, *, seq, tm):
    n, d = x.shape
    d_a = ln_g.shape[0]
    d_b = pool_scale.shape[0]
    n_heads, chunk, _ = w_s.shape
    halo = max(B_WINDOWS)
    assert len(B_WINDOWS) == w_pool.shape[0] and n_heads * A_HEAD == d_a
    assert tm % chunk == 0 and seq % tm == 0 and halo % SUBLANES == 0 and halo <= tm
    kern = functools.partial(_even_mixer_kernel, tiles_per_seq=seq // tm, d_a=d_a)
    full = lambda *shape: pl.BlockSpec(shape, lambda i: (0,) * len(shape))
    return pl.pallas_call(
        kern,
        grid=(n // tm,),
        in_specs=[
            pl.BlockSpec((tm, d), lambda i: (i, 0)),
            full(d, 2 * d_a + d_b),
            full(1, d_a),
            full(1, d_a),
            full(*w_s.shape),
            full(chunk, n_heads),
            full(*w_pool.shape),
            full(1, d_b),
        ],
        out_specs=pl.BlockSpec((tm, d_a + d_b), lambda i: (i, 0)),
        out_shape=jax.ShapeDtypeStruct((n, d_a + d_b), BF16),
        scratch_shapes=[pltpu.VMEM((halo + tm, d_b), F32)],
        compiler_params=pltpu.CompilerParams(
            dimension_semantics=("arbitrary",), vmem_limit_bytes=VMEM_LIMIT_BYTES),
        name="even_mixer",
    )(x, w_in, ln_g.reshape(1, d_a), ln_b.reshape(1, d_a), w_s, b_s.T, w_pool,
      pool_scale.reshape(1, d_b))


def _split3(a):
    hi = a.astype(BF16)
    r = a - hi.astype(F32)
    mid = r.astype(BF16)
    lo = (r - mid.astype(F32)).astype(BF16)
    return hi, mid, lo


def _odd_pipe_kernel(xb_ref, w_ref, lbng_ref, y_ref,
                     p0_ref, p1_ref, st_ref, *, nj, tiles_per_seq, layer):
    s = pl.program_id(0)
    tm = xb_ref.shape[0]
    n_parts = 4
    hw = w_ref.shape[1] // n_parts
    heads = hw // C_HEAD
    kt = MXU_DIM
    n_k = xb_ref.shape[1] // kt
    n_chunks = tm // C_CHUNK
    prev = jnp.maximum(s - 1, 0)
    ip = prev // nj
    jp = prev % nj

    @pl.when(s == 0)
    def _():
        p1_ref[...] = jnp.zeros(p1_ref.shape, F32)

    @pl.when(ip % tiles_per_seq == 0)
    def _():
        for hd in range(heads):
            st_ref[jp * heads + hd] = jnp.zeros((C_HEAD, C_HEAD), F32)

    def stage(pw_ref, pr_ref):
        depth = lbng_ref.shape[0] - 1
        lbp = lbng_ref[0:depth, :]
        e = jnp.exp(lbp - jnp.max(lbp, axis=0, keepdims=True))
        sm = e / jnp.sum(e, axis=0, keepdims=True)
        lb = jnp.zeros((1, hw), F32)
        for r in range(1, layer + 1):
            lb = lb + sm[r:r + 1, :]
        ng = lbng_ref[depth:depth + 1, :]
        row = lax.broadcasted_iota(jnp.int32, (C_CHUNK, C_CHUNK), 0)
        col = lax.broadcasted_iota(jnp.int32, (C_CHUNK, C_CHUNK), 1)
        causal = col <= row
        tri = causal.astype(BF16)

        pieces = [(part, k) for part in range(n_parts) for k in range(n_k)]
        n_pieces = len(pieces)
        n_slots = 3 * n_chunks
        partial = {}
        slot = [0]

        def project():
            t = slot[0]
            slot[0] += 1
            for _ in range((t + 1) * n_pieces // n_slots - t * n_pieces // n_slots):
                part, k = pieces.pop(0)
                ks = slice(k * kt, (k + 1) * kt)
                ps = slice(part * hw, (part + 1) * hw)
                d = _dot(xb_ref[:, ks], w_ref[ks, ps])
                partial[part] = d if k == 0 else partial[part] + d
                if k == n_k - 1:
                    pw_ref[:, ps] = partial.pop(part)

        head_cols = [slice(hd * C_HEAD, (hd + 1) * C_HEAD) for hd in range(heads)]

        def decays(c):
            rs = slice(c * C_CHUNK, (c + 1) * C_CHUNK)
            f = lb + (1.0 - lb) * jax.nn.sigmoid(pr_ref[rs, hw:2 * hw])
            kk = 1.0 - f
            hi, mid, lo = _split3(jnp.log(f))
            bcum = _dot(tri, hi) + _dot(tri, mid) + _dot(tri, lo)
            blast = bcum[C_CHUNK - 1:C_CHUNK, :]
            return dict(
                q_dec=(jax.nn.silu(pr_ref[rs, 0:hw]) * jnp.exp(bcum)).astype(BF16),
                k_dec=(kk * jnp.exp(-bcum)).astype(BF16),
                k_end=(kk * jnp.exp(blast - bcum)).astype(BF16),
                dec=jnp.exp(blast),
                vb=pr_ref[rs, 2 * hw:3 * hw].astype(BF16),
                gate=jax.nn.sigmoid(pr_ref[rs, 3 * hw:4 * hw]))

        def scores(p):
            att = [jnp.where(causal, _dot_nt(p["q_dec"][:, cs], p["k_dec"][:, cs]), 0.0).astype(BF16)
                   for cs in head_cols]
            upd = [_dot_tn(p["vb"][:, cs], p["k_end"][:, cs]) for cs in head_cols]
            return att, upd

        def outputs(c, p, att, upd):
            rs = slice(c * C_CHUNK, (c + 1) * C_CHUNK)
            for hd, cs in enumerate(head_cols):
                st = states[hd]
                o = _dot(att[hd], p["vb"][:, cs]) + _dot_nt(p["q_dec"][:, cs], st.astype(BF16))
                states[hd] = p["dec"][:, cs] * st + upd[hd]
                o = o * lax.rsqrt(jnp.mean(jnp.square(o), axis=-1, keepdims=True) + LN_EPS) * ng[:, cs]
                y_ref[rs, cs] = (o * p["gate"][:, cs]).astype(BF16)

        states = [st_ref[jp * heads + hd] for hd in range(heads)]
        dec_out, sc_out = {}, {}
        for t in range(n_chunks + 2):
            if t < n_chunks:
                project()
                dec_out[t] = decays(t)
            if 0 <= t - 1 < n_chunks:
                project()
                sc_out[t - 1] = scores(dec_out[t - 1])
            if 0 <= t - 2 < n_chunks:
                project()
                outputs(t - 2, dec_out.pop(t - 2), *sc_out.pop(t - 2))
        assert not pieces and not partial
        for hd in range(heads):
            st_ref[jp * heads + hd] = states[hd]

    @pl.when(s % 2 == 0)
    def _():
        stage(p0_ref, p1_ref)

    @pl.when(s % 2 == 1)
    def _():
        stage(p1_ref, p0_ref)


def _odd_pipe(xb, w_in_t, lb_param, norm_g, *, layer, seq, tm):
    n, d = xb.shape
    d_c = norm_g.shape[0]
    depth = lb_param.shape[0]
    nj, hw = w_in_t.shape[0], w_in_t.shape[2] // 4
    nm = n // tm
    assert hw % C_HEAD == 0 and nj * hw == d_c and tm % C_CHUNK == 0 and seq % tm == 0
    assert d % MXU_DIM == 0 and n % tm == 0
    kern = functools.partial(_odd_pipe_kernel, nj=nj, tiles_per_seq=seq // tm, layer=layer)
    last = nm * nj - 1
    cur = lambda s: jnp.minimum(s, last)
    prv = lambda s: jnp.maximum(s - 1, 0)
    return pl.pallas_call(
        kern,
        grid=(nm * nj + 1,),
        in_specs=[
            pl.BlockSpec((tm, d), lambda s: (cur(s) // nj, 0)),
            pl.BlockSpec((None, d, 4 * hw), lambda s: (cur(s) % nj, 0, 0)),
            pl.BlockSpec((depth + 1, hw), lambda s: (0, prv(s) % nj)),
        ],
        out_specs=pl.BlockSpec((tm, hw), lambda s: (prv(s) // nj, prv(s) % nj)),
        out_shape=jax.ShapeDtypeStruct((n, d_c), BF16),
        scratch_shapes=[
            pltpu.VMEM((tm, 4 * hw), F32),
            pltpu.VMEM((tm, 4 * hw), F32),
            pltpu.VMEM((d_c // C_HEAD, C_HEAD, C_HEAD), F32),
        ],
        compiler_params=pltpu.CompilerParams(
            dimension_semantics=("arbitrary",), vmem_limit_bytes=VMEM_LIMIT_BYTES),
        name="odd_mixer",
    )(xb, w_in_t, jnp.concatenate([lb_param, norm_g.reshape(1, d_c)], axis=0))


def kernel(x, ev_w_in, ev_ln_v_g, ev_ln_v_b, ev_w_s, ev_b_s, ev_w_pool, ev_pool_scale,
           ev_w_out, od_w_in, od_norm_g, od_w_out, lb_param, ffn_w_up, ffn_conv_w,
           ffn_conv_b, ffn_w_down, ln1_g, ln1_b, ln2_g, ln2_b):
    bn, seq, d = x.shape
    depth = ln1_g.shape[0]
    alpha = (2 * depth) ** 0.25
    xf = x.reshape(bn * seq, d)
    tf = FFN_COL_TILE
    nf = ffn_w_down.shape[1] // tf
    ffn_w_up_t = _tile_columns(ffn_w_up, 2, nf, tf, BF16)
    ffn_conv_p = _tile_columns_small(
        jnp.concatenate([ffn_conv_w, ffn_conv_b[:, None, :]], axis=1), 2, nf, tf)
    ffn_w_down_b = ffn_w_down.astype(BF16)
    for l in range(depth):
        if l % 2 == 0:
            e = l // 2
            y = _even_mixer(xf, ev_w_in[e].astype(BF16), ev_ln_v_g[e], ev_ln_v_b[e], ev_w_s[e],
                            ev_b_s[e], ev_w_pool[e].astype(BF16), ev_pool_scale[e],
                            seq=seq, tm=min(EVEN_ROW_TILE, seq))
            w_out = ev_w_out[e]
        else:
            o = l // 2
            hw = ODD_GROUP_WIDTH
            w_in_t = _tile_columns(od_w_in[o:o + 1], 4, od_norm_g.shape[1] // hw, hw, BF16)[0]
            y = _odd_pipe(xb, w_in_t, lb_param, od_norm_g[o], layer=l, seq=seq, tm=min(ROW_TILE, seq))
            w_out = od_w_out[o]
        xf, xb = _proj_ln(y, w_out.astype(BF16), xf, ln1_g[l], ln1_b[l], alpha=alpha,
                          tm=min(ROW_TILE, seq))
        xf, xb = _conv_ffn(xf, xb, ffn_w_up_t, ffn_conv_p, ffn_w_down_b, ln2_g[l], ln2_b[l],
                           layer=l, alpha=alpha, seq=seq, tm=min(ROW_TILE, seq))
    return xf.reshape(bn, seq, d)
```

```python
import functools

import jax
import jax.numpy as jnp
from jax import lax
from jax.experimental import pallas as pl
from jax.experimental.pallas import tpu as pltpu

F32 = jnp.float32
BF16 = jnp.bfloat16

A_HEAD = 128
B_WINDOWS = (2, 4, 8, 16)
C_HEAD = 128
C_CHUNK = 64
LN_EPS = 1e-5

SUBLANES = 8
MXU_DIM = 256
UP_PIECE_K = MXU_DIM
ROW_TILE = 512
EVEN_ROW_TILE = 256
FFN_COL_TILE = 512
ODD_GROUP_WIDTH = 2 * C_HEAD
GATE_ROWS = 16
VMEM_LIMIT_BYTES = 56 * 1024 * 1024


def _dot(a, b):
    return jnp.dot(a, b, preferred_element_type=F32)


def _dot_nt(a, b):
    return lax.dot_general(a, b, (((1,), (1,)), ((), ())), preferred_element_type=F32)


def _dot_tn(a, b):
    return lax.dot_general(a, b, (((0,), (0,)), ((), ())), preferred_element_type=F32)


def _layer_norm(z, g, b):
    mu = jnp.mean(z, axis=-1, keepdims=True)
    zc = z - mu
    var = jnp.mean(jnp.square(zc), axis=-1, keepdims=True)
    return zc * lax.rsqrt(var + LN_EPS) * g + b


def _proj_ln_kernel(y_ref, w_ref, x_ref, g_ref, b_ref, o_ref, ob_ref, wb_ref, *, alpha):
    @pl.when(pl.program_id(0) == 0)
    def _():
        wb_ref[...] = w_ref[...].astype(BF16)

    mix = _dot(y_ref[...], wb_ref[...])
    out = _layer_norm(alpha * x_ref[...] + mix, g_ref[...], b_ref[...])
    o_ref[...] = out
    ob_ref[...] = out.astype(BF16)


def _proj_ln(y, w, x, g, b, *, alpha, tm):
    n, d = x.shape
    k = y.shape[1]
    return pl.pallas_call(
        functools.partial(_proj_ln_kernel, alpha=alpha),
        grid=(n // tm,),
        in_specs=[
            pl.BlockSpec((tm, k), lambda i: (i, 0)),
            pl.BlockSpec((k, d), lambda i: (0, 0), pipeline_mode=pl.Buffered(1)),
            pl.BlockSpec((tm, d), lambda i: (i, 0)),
            pl.BlockSpec((1, d), lambda i: (0, 0)),
            pl.BlockSpec((1, d), lambda i: (0, 0)),
        ],
        out_specs=[pl.BlockSpec((tm, d), lambda i: (i, 0)), pl.BlockSpec((tm, d), lambda i: (i, 0))],
        out_shape=[jax.ShapeDtypeStruct((n, d), F32), jax.ShapeDtypeStruct((n, d), BF16)],
        scratch_shapes=[pltpu.VMEM((k, d), BF16)],
        compiler_params=pltpu.CompilerParams(
            dimension_semantics=("arbitrary",), vmem_limit_bytes=VMEM_LIMIT_BYTES),
        name="proj_ln",
    )(y, w, x, g.reshape(1, d), b.reshape(1, d))


def _conv_ffn_kernel(xb_ref, xr_ref, wu_ref, cp_ref, wd_ref,
                     g_ref, b_ref, o_ref, ob_ref, acc_ref, h0_ref, h1_ref, g0_ref, g1_ref, carry_ref,
                     *, alpha, nf, tiles_per_seq, taps):
    s = pl.program_id(0)
    tm = xb_ref.shape[0]
    tf = wu_ref.shape[1] // 2
    halo = SUBLANES
    i = s // nf
    j = s % nf

    @pl.when(s == 0)
    def _():
        h1_ref[...] = jnp.zeros(h1_ref.shape, F32)
        g0_ref[...] = jnp.zeros(g0_ref.shape, BF16)
        acc_ref[...] = jnp.zeros(acc_ref.shape, F32)
        carry_ref[...] = jnp.zeros(carry_ref.shape, F32)

    def stage(hw_ref, hr_ref, gw_ref, gr_ref):
        seq_start = (i % tiles_per_seq) == 0
        d_in = xb_ref.shape[1]
        kt = UP_PIECE_K
        n_k = d_in // kt
        rb = GATE_ROWS
        per_piece = tm // rb // (2 * n_k)
        d_out = acc_ref.shape[1]

        def gate(blk):
            r0 = blk * rb

            def conv(cs):
                cp = cp_ref[:, cs]
                hc = cp[taps:taps + 1, :] + cp[taps - 1:taps, :] * hr_ref[halo + r0:halo + r0 + rb, cs]
                for tap in range(taps - 1):
                    sh = taps - 1 - tap
                    hc = hc + cp[tap:tap + 1, :] * hr_ref[halo + r0 - sh:halo + r0 - sh + rb, cs]
                return hc

            a = conv(slice(0, tf))
            v = conv(slice(tf, 2 * tf))
            gw_ref[r0:r0 + rb, :] = (jax.nn.silu(a) * v).astype(BF16)

        down_blocks = list(range(0, d_out, MXU_DIM))
        n_down = len(down_blocks)
        n_up = 2 * n_k

        def down(n_blocks):
            for _ in range(n_blocks):
                n0 = down_blocks.pop(0)
                ns = slice(n0, n0 + MXU_DIM)
                acc_ref[:, ns] += _dot(gr_ref[...], wd_ref[:, ns])

        blk = 0
        piece = 0
        for half in range(2):
            cs = slice(half * tf, (half + 1) * tf)
            h = None
            for k in range(n_k):
                ks = slice(k * kt, (k + 1) * kt)
                part = _dot(xb_ref[:, ks], wu_ref[ks, cs])
                h = part if h is None else h + part
                for _ in range(per_piece):
                    gate(blk)
                    blk += 1
                piece += 1
                down(piece * n_down // n_up - (piece - 1) * n_down // n_up)
            hw_ref[0:halo, cs] = jnp.where(seq_start, 0.0, carry_ref[j, :, cs])
            hw_ref[halo:halo + tm, cs] = h
            carry_ref[j, :, cs] = h[tm - halo:, :]
        assert not down_blocks

    @pl.when(s % 2 == 0)
    def _():
        stage(h0_ref, h1_ref, g1_ref, g0_ref)

    @pl.when(s % 2 == 1)
    def _():
        stage(h1_ref, h0_ref, g0_ref, g1_ref)

    @pl.when(s < 2)
    def _():
        acc_ref[...] = jnp.zeros(acc_ref.shape, F32)

    @pl.when((s >= 2) & ((s - 2) % nf == nf - 1))
    def _():
        out = _layer_norm(alpha * xr_ref[...] + acc_ref[...], g_ref[...], b_ref[...])
        o_ref[...] = out
        ob_ref[...] = out.astype(BF16)
        acc_ref[...] = jnp.zeros(acc_ref.shape, F32)


def _tile_cast_kernel(a_ref, o_ref):
    o_ref[...] = a_ref[...].astype(o_ref.dtype)


def _tile_columns(a, parts, nt, tw, dtype):
    n_l, rows, cols = a.shape
    assert cols == parts * nt * tw
    return pl.pallas_call(
        _tile_cast_kernel,
        grid=(n_l, nt, parts),
        in_specs=[pl.BlockSpec((None, rows, tw), lambda l, j, p: (l, 0, p * nt + j))],
        out_specs=pl.BlockSpec((None, None, rows, tw), lambda l, j, p: (l, j, 0, p)),
        out_shape=jax.ShapeDtypeStruct((n_l, nt, rows, parts * tw), dtype),
        compiler_params=pltpu.CompilerParams(
            dimension_semantics=("arbitrary", "arbitrary", "arbitrary"), vmem_limit_bytes=VMEM_LIMIT_BYTES),
        name="tile_cast",
    )(a)


def _tile_rows_kernel(a_ref, o_ref, *, parts, nt, tw):
    for j in range(nt):
        for p in range(parts):
            c0 = (p * nt + j) * tw
            o_ref[j, :, p * tw:(p + 1) * tw] = a_ref[:, c0:c0 + tw]


def _tile_columns_small(a, parts, nt, tw):
    n_l, rows, cols = a.shape
    assert cols == parts * nt * tw
    return pl.pallas_call(
        functools.partial(_tile_rows_kernel, parts=parts, nt=nt, tw=tw),
        grid=(n_l,),
        in_specs=[pl.BlockSpec((None, rows, cols), lambda l: (l, 0, 0))],
        out_specs=pl.BlockSpec((None, nt, rows, parts * tw), lambda l: (l, 0, 0, 0)),
        out_shape=jax.ShapeDtypeStruct((n_l, nt, rows, parts * tw), a.dtype),
        compiler_params=pltpu.CompilerParams(
            dimension_semantics=("arbitrary",), vmem_limit_bytes=VMEM_LIMIT_BYTES),
        name="tile_rows",
    )(a)


def _conv_ffn(x, xb, w_up_t, conv_p_t, w_down, g, b, *, layer, alpha, seq, tm):
    n, d = x.shape
    nf, tf = w_up_t.shape[1], w_up_t.shape[3] // 2
    nm = n // tm
    taps = conv_p_t.shape[2] - 1
    halo = SUBLANES
    assert taps - 1 <= halo and seq % tm == 0 and n % tm == 0 and w_down.shape[1] == nf * tf
    kern = functools.partial(_conv_ffn_kernel, alpha=alpha, nf=nf, tiles_per_seq=seq // tm, taps=taps)
    last = nm * nf - 1
    up_tile = lambda s: jnp.minimum(s, last) // nf
    up_col = lambda s: jnp.minimum(s, last) % nf
    conv_col = lambda s: jnp.clip(s - 1, 0, last) % nf
    down_tile = lambda s: jnp.maximum(s - 2, 0) // nf
    down_col = lambda s: jnp.maximum(s - 2, 0) % nf
    return pl.pallas_call(
        kern,
        grid=(nm * nf + 2,),
        in_specs=[
            pl.BlockSpec((tm, d), lambda s: (up_tile(s), 0)),
            pl.BlockSpec((tm, d), lambda s: (down_tile(s), 0)),
            pl.BlockSpec((None, None, d, 2 * tf), lambda s: (layer, up_col(s), 0, 0)),
            pl.BlockSpec((None, None, taps + 1, 2 * tf), lambda s: (layer, conv_col(s), 0, 0)),
            pl.BlockSpec((None, tf, d), lambda s: (layer, down_col(s), 0)),
            pl.BlockSpec((1, d), lambda s: (0, 0)),
            pl.BlockSpec((1, d), lambda s: (0, 0)),
        ],
        out_specs=[pl.BlockSpec((tm, d), lambda s: (down_tile(s), 0)),
                   pl.BlockSpec((tm, d), lambda s: (down_tile(s), 0))],
        out_shape=[jax.ShapeDtypeStruct((n, d), F32), jax.ShapeDtypeStruct((n, d), BF16)],
        scratch_shapes=[
            pltpu.VMEM((tm, d), F32),
            pltpu.VMEM((halo + tm, 2 * tf), F32),
            pltpu.VMEM((halo + tm, 2 * tf), F32),
            pltpu.VMEM((tm, tf), BF16),
            pltpu.VMEM((tm, tf), BF16),
            pltpu.VMEM((nf, halo, 2 * tf), F32),
        ],
        compiler_params=pltpu.CompilerParams(
            dimension_semantics=("arbitrary",), vmem_limit_bytes=VMEM_LIMIT_BYTES),
        name="conv_ffn",
    )(xb, x, w_up_t, conv_p_t, w_down, g.reshape(1, d), b.reshape(1, d))


def _even_mixer_kernel(x_ref, win_ref, lng_ref, lnb_ref, ws_ref, bst_ref, wp_ref, ps_ref,
                       y_ref, ext_ref, *, tiles_per_seq, d_a):
    i = pl.program_id(0)
    tm = x_ref.shape[0]
    n_heads = ws_ref.shape[0]
    chunk = ws_ref.shape[1]
    n_groups = wp_ref.shape[0]
    grp = wp_ref.shape[1]
    halo = ext_ref.shape[0] - tm

    h = _dot(x_ref[...].astype(BF16), win_ref[...])
    za = jax.nn.gelu(h[:, :2 * d_a])
    u = za[:, :d_a]
    v = _layer_norm(za[:, d_a:], lng_ref[...], lnb_ref[...]).astype(BF16)

    row = lax.broadcasted_iota(jnp.int32, (chunk, chunk), 0)
    col = lax.broadcasted_iota(jnp.int32, (chunk, chunk), 1)
    causal = col <= row
    bst = bst_ref[...]
    for hd in range(n_heads):
        w = jnp.where(causal, ws_ref[hd], 0.0).astype(BF16)
        cs = slice(hd * A_HEAD, (hd + 1) * A_HEAD)
        for c in range(tm // chunk):
            rs = slice(c * chunk, (c + 1) * chunk)
            s = _dot(w, v[rs, cs]) + bst[:, hd:hd + 1]
            y_ref[rs, cs] = (u[rs, cs] * s).astype(BF16)

    xb = h[:, 2 * d_a:]

    @pl.when(i % tiles_per_seq == 0)
    def _():
        ext_ref[0:halo, :] = jnp.zeros((halo, xb.shape[1]), F32)

    @pl.when(i % tiles_per_seq != 0)
    def _():
        ext_ref[0:halo, :] = ext_ref[tm:tm + halo, :]

    ext_ref[halo:halo + tm, :] = xb
    pos = (i % tiles_per_seq) * tm + lax.broadcasted_iota(jnp.int32, (tm, 1), 0)
    posf = (pos + 1).astype(F32)
    for gi in range(n_groups):
        win = B_WINDOWS[gi]
        cs = slice(gi * grp, (gi + 1) * grp)
        wsum = xb[:, cs]
        for k in range(1, win):
            wsum = wsum + ext_ref[halo - k:halo - k + tm, cs]
        cnt = jnp.minimum(posf, float(win))
        p = wsum / cnt - xb[:, cs]
        yb = _dot(p.astype(BF16), wp_ref[gi]) * ps_ref[:, cs]
        y_ref[:, d_a + gi * grp:d_a + (gi + 1) * grp] = yb.astype(BF16)


def _even_mixer(x, w_in, ln_g, ln_b, w_s, b_s, w_pool, pool_scale, *, seq, tm):
    n, d = x.shape
    d_a = ln_g.shape[0]
    d_b = pool_scale.shape[0]
    n_heads, chunk, _ = w_s.shape
    halo = max(B_WINDOWS)
    assert len(B_WINDOWS) == w_pool.shape[0] and n_heads * A_HEAD == d_a
    assert tm % chunk == 0 and seq % tm == 0 and halo % SUBLANES == 0 and halo <= tm
    kern = functools.partial(_even_mixer_kernel, tiles_per_seq=seq // tm, d_a=d_a)
    full = lambda *shape: pl.BlockSpec(shape, lambda i: (0,) * len(shape))
    return pl.pallas_call(
        kern,
        grid=(n // tm,),
        in_specs=[
            pl.BlockSpec((tm, d), lambda i: (i, 0)),
            full(d, 2 * d_a + d_b),
            full(1, d_a),
            full(1, d_a),
            full(*w_s.shape),
            full(chunk, n_heads),
            full(*w_pool.shape),
            full(1, d_b),
        ],
        out_specs=pl.BlockSpec((tm, d_a + d_b), lambda i: (i, 0)),
        out_shape=jax.ShapeDtypeStruct((n, d_a + d_b), BF16),
        scratch_shapes=[pltpu.VMEM((halo + tm, d_b), F32)],
        compiler_params=pltpu.CompilerParams(
            dimension_semantics=("arbitrary",), vmem_limit_bytes=VMEM_LIMIT_BYTES),
        name="even_mixer",
    )(x, w_in, ln_g.reshape(1, d_a), ln_b.reshape(1, d_a), w_s, b_s.T, w_pool,
      pool_scale.reshape(1, d_b))


def _split3(a):
    hi = a.astype(BF16)
    r = a - hi.astype(F32)
    mid = r.astype(BF16)
    lo = (r - mid.astype(F32)).astype(BF16)
    return hi, mid, lo


def _odd_pipe_kernel(xb_ref, w_ref, lbng_ref, y_ref,
                     p0_ref, p1_ref, st_ref, *, nj, tiles_per_seq, layer):
    s = pl.program_id(0)
    tm = xb_ref.shape[0]
    n_parts = 4
    hw = w_ref.shape[1] // n_parts
    heads = hw // C_HEAD
    kt = MXU_DIM
    n_k = xb_ref.shape[1] // kt
    n_chunks = tm // C_CHUNK
    prev = jnp.maximum(s - 1, 0)
    ip = prev // nj
    jp = prev % nj

    @pl.when(s == 0)
    def _():
        p1_ref[...] = jnp.zeros(p1_ref.shape, F32)

    @pl.when(ip % tiles_per_seq == 0)
    def _():
        for hd in range(heads):
            st_ref[jp * heads + hd] = jnp.zeros((C_HEAD, C_HEAD), F32)

    def stage(pw_ref, pr_ref):
        depth = lbng_ref.shape[0] - 1
        lbp = lbng_ref[0:depth, :]
        e = jnp.exp(lbp - jnp.max(lbp, axis=0, keepdims=True))
        sm = e / jnp.sum(e, axis=0, keepdims=True)
        lb = jnp.zeros((1, hw), F32)
        for r in range(1, layer + 1):
            lb = lb + sm[r:r + 1, :]
        ng = lbng_ref[depth:depth + 1, :]
        row = lax.broadcasted_iota(jnp.int32, (C_CHUNK, C_CHUNK), 0)
        col = lax.broadcasted_iota(jnp.int32, (C_CHUNK, C_CHUNK), 1)
        causal = col <= row
        tri = causal.astype(BF16)

        pieces = [(part, k) for part in range(n_parts) for k in range(n_k)]
        n_pieces = len(pieces)
        n_slots = 3 * n_chunks
        partial = {}
        slot = [0]

        def project():
            t = slot[0]
            slot[0] += 1
            for _ in range((t + 1) * n_pieces // n_slots - t * n_pieces // n_slots):
                part, k = pieces.pop(0)
                ks = slice(k * kt, (k + 1) * kt)
                ps = slice(part * hw, (part + 1) * hw)
                d = _dot(xb_ref[:, ks], w_ref[ks, ps])
                partial[part] = d if k == 0 else partial[part] + d
                if k == n_k - 1:
                    pw_ref[:, ps] = partial.pop(part)

        head_cols = [slice(hd * C_HEAD, (hd + 1) * C_HEAD) for hd in range(heads)]

        def decays(c):
            rs = slice(c * C_CHUNK, (c + 1) * C_CHUNK)
            f = lb + (1.0 - lb) * jax.nn.sigmoid(pr_ref[rs, hw:2 * hw])
            kk = 1.0 - f
            hi, mid, lo = _split3(jnp.log(f))
            bcum = _dot(tri, hi) + _dot(tri, mid) + _dot(tri, lo)
            blast = bcum[C_CHUNK - 1:C_CHUNK, :]
            return dict(
                q_dec=(jax.nn.silu(pr_ref[rs, 0:hw]) * jnp.exp(bcum)).astype(BF16),
                k_dec=(kk * jnp.exp(-bcum)).astype(BF16),
                k_end=(kk * jnp.exp(blast - bcum)).astype(BF16),
                dec=jnp.exp(blast),
                vb=pr_ref[rs, 2 * hw:3 * hw].astype(BF16),
                gate=jax.nn.sigmoid(pr_ref[rs, 3 * hw:4 * hw]))

        def scores(p):
            att = [jnp.where(causal, _dot_nt(p["q_dec"][:, cs], p["k_dec"][:, cs]), 0.0).astype(BF16)
                   for cs in head_cols]
            upd = [_dot_tn(p["vb"][:, cs], p["k_end"][:, cs]) for cs in head_cols]
            return att, upd

        def outputs(c, p, att, upd):
            rs = slice(c * C_CHUNK, (c + 1) * C_CHUNK)
            for hd, cs in enumerate(head_cols):
                st = states[hd]
                o = _dot(att[hd], p["vb"][:, cs]) + _dot_nt(p["q_dec"][:, cs], st.astype(BF16))
                states[hd] = p["dec"][:, cs] * st + upd[hd]
                o = o * lax.rsqrt(jnp.mean(jnp.square(o), axis=-1, keepdims=True) + LN_EPS) * ng[:, cs]
                y_ref[rs, cs] = (o * p["gate"][:, cs]).astype(BF16)

        states = [st_ref[jp * heads + hd] for hd in range(heads)]
        dec_out, sc_out = {}, {}
        for t in range(n_chunks + 2):
            if t < n_chunks:
                project()
                dec_out[t] = decays(t)
            if 0 <= t - 1 < n_chunks:
                project()
                sc_out[t - 1] = scores(dec_out[t - 1])
            if 0 <= t - 2 < n_chunks:
                project()
                outputs(t - 2, dec_out.pop(t - 2), *sc_out.pop(t - 2))
        assert not pieces and not partial
        for hd in range(heads):
            st_ref[jp * heads + hd] = states[hd]

    @pl.when(s % 2 == 0)
    def _():
        stage(p0_ref, p1_ref)

    @pl.when(s % 2 == 1)
    def _():
        stage(p1_ref, p0_ref)


def _odd_pipe(xb, w_in_t, lb_param, norm_g, *, layer, seq, tm):
    n, d = xb.shape
    d_c = norm_g.shape[0]
    depth = lb_param.shape[0]
    nj, hw = w_in_t.shape[0], w_in_t.shape[2] // 4
    nm = n // tm
    assert hw % C_HEAD == 0 and nj * hw == d_c and tm % C_CHUNK == 0 and seq % tm == 0
    assert d % MXU_DIM == 0 and n % tm == 0
    kern = functools.partial(_odd_pipe_kernel, nj=nj, tiles_per_seq=seq // tm, layer=layer)
    last = nm * nj - 1
    cur = lambda s: jnp.minimum(s, last)
    prv = lambda s: jnp.maximum(s - 1, 0)
    return pl.pallas_call(
        kern,
        grid=(nm * nj + 1,),
        in_specs=[
            pl.BlockSpec((tm, d), lambda s: (cur(s) // nj, 0)),
            pl.BlockSpec((None, d, 4 * hw), lambda s: (cur(s) % nj, 0, 0)),
            pl.BlockSpec((depth + 1, hw), lambda s: (0, prv(s) % nj)),
        ],
        out_specs=pl.BlockSpec((tm, hw), lambda s: (prv(s) // nj, prv(s) % nj)),
        out_shape=jax.ShapeDtypeStruct((n, d_c), BF16),
        scratch_shapes=[
            pltpu.VMEM((tm, 4 * hw), F32),
            pltpu.VMEM((tm, 4 * hw), F32),
            pltpu.VMEM((d_c // C_HEAD, C_HEAD, C_HEAD), F32),
        ],
        compiler_params=pltpu.CompilerParams(
            dimension_semantics=("arbitrary",), vmem_limit_bytes=VMEM_LIMIT_BYTES),
        name="odd_mixer",
    )(xb, w_in_t, jnp.concatenate([lb_param, norm_g.reshape(1, d_c)], axis=0))


def kernel(x, ev_w_in, ev_ln_v_g, ev_ln_v_b, ev_w_s, ev_b_s, ev_w_pool, ev_pool_scale,
           ev_w_out, od_w_in, od_norm_g, od_w_out, lb_param, ffn_w_up, ffn_conv_w,
           ffn_conv_b, ffn_w_down, ln1_g, ln1_b, ln2_g, ln2_b):
    bn, seq, d = x.shape
    depth = ln1_g.shape[0]
    alpha = (2 * depth) ** 0.25
    xf = x.reshape(bn * seq, d)
    tf = FFN_COL_TILE
    nf = ffn_w_down.shape[1] // tf
    ffn_w_up_t = _tile_columns(ffn_w_up, 2, nf, tf, BF16)
    ffn_conv_p = _tile_columns_small(
        jnp.concatenate([ffn_conv_w, ffn_conv_b[:, None, :]], axis=1), 2, nf, tf)
    ffn_w_down_b = ffn_w_down.astype(BF16)
    for l in range(depth):
        if l % 2 == 0:
            e = l // 2
            y = _even_mixer(xf, ev_w_in[e].astype(BF16), ev_ln_v_g[e], ev_ln_v_b[e], ev_w_s[e],
                            ev_b_s[e], ev_w_pool[e].astype(BF16), ev_pool_scale[e],
                            seq=seq, tm=min(EVEN_ROW_TILE, seq))
            w_out = ev_w_out[e]
        else:
            o = l // 2
            hw = ODD_GROUP_WIDTH
            w_in_t = _tile_columns(od_w_in[o:o + 1], 4, od_norm_g.shape[1] // hw, hw, BF16)[0]
            y = _odd_pipe(xb, w_in_t, lb_param, od_norm_g[o], layer=l, seq=seq, tm=min(ROW_TILE, seq))
            w_out = od_w_out[o]
        xf, xb = _proj_ln(y, w_out, xf, ln1_g[l], ln1_b[l], alpha=alpha,
                          tm=min(ROW_TILE, seq))
        xf, xb = _conv_ffn(xf, xb, ffn_w_up_t, ffn_conv_p, ffn_w_down_b, ln2_g[l], ln2_b[l],
                           layer=l, alpha=alpha, seq=seq, tm=min(ROW_TILE, seq))
    return xf.reshape(bn, seq, d)
```
